```python
import math
import jax
import jax.numpy as jnp
from jax import lax
import numpy as np

D_MODEL = 1024
BATCH = 4
SEQ = 8192
DEPTH = 4

GRID_W = 64
CTX_LEN = 256
N_MIXERS = 3
N_GDN_LAYERS = (DEPTH + 2) // 3
N_LRU_LAYERS = (DEPTH + 1) // 3
N_HG_LAYERS = DEPTH // 3
EPS = 1e-6
CONV_W = 4
CONV_PAD = (2, 1)
FFN_HIDDEN = -(-(8 * D_MODEL) // (3 * 256)) * 256

GDN_HEADS = 8
GDN_DK = 128
GDN_DV = 128
GDN_CHUNK = 64
GDN_QKV = GDN_HEADS * (2 * GDN_DK + GDN_DV)
GDN_Z = GDN_HEADS * GDN_DV
GDN_IN = GDN_QKV + GDN_Z + 4 * GDN_HEADS

LRU_WIDTH = D_MODEL
LRU_BLOCKS = 8
LRU_BW = LRU_WIDTH // LRU_BLOCKS
LRU_C = 8.0

HG_HEADS = 8
HG_DK = 128
HG_DV = D_MODEL // HG_HEADS
HG_CHUNK = 64
HG_QK = HG_HEADS * HG_DK
HG_V = HG_HEADS * HG_DV
HG_IN = 3 * HG_QK + 2 * HG_V

kernel_name = 'hybrid_gdn_rglru_hgrn2_flow_block'


def rmsnorm(x, w):
    xf = x.astype(jnp.float32)
    y = xf * lax.rsqrt(jnp.mean(xf * xf, axis=-1, keepdims=True) + EPS)
    return (y * w.astype(jnp.float32)).astype(x.dtype)


def l2norm(x):
    xf = x.astype(jnp.float32)
    return xf * lax.rsqrt(jnp.sum(xf * xf, axis=-1, keepdims=True) + EPS)


def modulate(h, shift, scale):
    return h * (1.0 + scale) + shift


def short_conv(x, w, b=None):
    L = x.shape[1]
    xp = jnp.pad(x, ((0, 0), CONV_PAD, (0, 0)))
    y = xp[:, 0:L] * w[0]
    for j in range(1, CONV_W):
        y = y + xp[:, j:j + L] * w[j]
    return y if b is None else y + b


def to_heads(t, n_heads):
    bsz, L, _ = t.shape
    return t.reshape(bsz, L, n_heads, -1).transpose(0, 2, 1, 3)


def flip(t, rev, axis):
    return jnp.flip(t, axis=axis) if rev else t


def to_scan_order(h, col):
    if not col:
        return h
    bsz, L, d = h.shape
    rows = L // GRID_W
    return h.reshape(bsz, rows, GRID_W, d).transpose(0, 2, 1, 3).reshape(bsz, L, d)


def from_scan_order(h, col):
    if not col:
        return h
    bsz, L, d = h.shape
    rows = L // GRID_W
    return h.reshape(bsz, GRID_W, rows, d).transpose(0, 2, 1, 3).reshape(bsz, L, d)


def gated_head_norm(o, z, w):
    bsz, nh, L, dv = o.shape
    y = rmsnorm(o.transpose(0, 2, 1, 3), w) * jax.nn.silu(z.reshape(bsz, L, nh, dv).astype(jnp.float32))
    return y.reshape(bsz, L, nh * dv).astype(z.dtype)


def swiglu(h, w1, w3, w2):
    return (jax.nn.silu(h @ w1) * (h @ w3)) @ w2


def gated_delta_rule(q, k, v, g, beta, s0):
    bsz, nh, L, dk = q.shape
    dv = v.shape[-1]
    C = GDN_CHUNK
    n = L // C
    q, k = (t.astype(jnp.float32).reshape(bsz, nh, n, C, dk) for t in (q, k))
    v = v.astype(jnp.float32).reshape(bsz, nh, n, C, dv)
    g, beta = (t.astype(jnp.float32).reshape(bsz, nh, n, C) for t in (g, beta))
    gcum = jnp.cumsum(g, axis=-1)
    incl = jnp.tril(jnp.ones((C, C), dtype=bool))
    strict = jnp.tril(jnp.ones((C, C), dtype=bool), -1)
    decay = jnp.exp(jnp.where(incl, gcum[..., :, None] - gcum[..., None, :], -jnp.inf))
    kb = k * beta[..., None]
    lmat = jnp.where(strict, jnp.einsum('bhnid,bhnjd->bhnij', kb, k) * decay, 0.0)
    eye = jnp.eye(C, dtype=jnp.float32)
    t_inv = lax.linalg.triangular_solve(eye + lmat, jnp.broadcast_to(eye, lmat.shape),
                                        left_side=True, lower=True, unit_diagonal=True)
    u = t_inv @ (v * beta[..., None])
    w = t_inv @ (kb * jnp.exp(gcum)[..., None])
    a_qk = jnp.einsum('bhnid,bhnjd->bhnij', q, k) * decay
    glast = gcum[..., -1]
    q_dec = q * jnp.exp(gcum)[..., None]
    k_dec = k * jnp.exp(glast[..., None] - gcum)[..., None]
    xs = tuple(jnp.moveaxis(t, 2, 0) for t in (q_dec, k_dec, u, w, a_qk, jnp.exp(glast)))

    def step(s, inp):
        qd, kd, uc, wc, aqk, dl = inp
        v_new = uc - wc @ s
        o = qd @ s + aqk @ v_new
        s = s * dl[..., None, None] + jnp.swapaxes(kd, -1, -2) @ v_new
        return s, o

    s_fin, o = lax.scan(step, s0, xs)
    return jnp.moveaxis(o, 0, 2).reshape(bsz, nh, L, dv), s_fin


def gdn_project(h, w_in, conv_w, a_log, dt_bias):
    bsz, L, _ = h.shape
    p = h @ w_in
    qkv = jax.nn.silu(short_conv(p[..., :GDN_QKV], conv_w))
    z = p[..., GDN_QKV:GDN_QKV + GDN_Z]
    ab = p[..., GDN_QKV + GDN_Z:].astype(jnp.float32).reshape(bsz, L, 2, 2, GDN_HEADS)
    hk = GDN_HEADS * GDN_DK
    q = l2norm(to_heads(qkv[..., :hk], GDN_HEADS)) * (GDN_DK ** -0.5)
    k = l2norm(to_heads(qkv[..., hk:2 * hk], GDN_HEADS))
    v = to_heads(qkv[..., 2 * hk:], GDN_HEADS).astype(jnp.float32)
    g = -jnp.exp(a_log.astype(jnp.float32)) * jax.nn.softplus(ab[:, :, 0] + dt_bias.astype(jnp.float32))
    beta = jax.nn.sigmoid(ab[:, :, 1])
    return q, k, v, g.transpose(2, 0, 3, 1), beta.transpose(2, 0, 3, 1), z


def gdn_mixer(hc, hl, w_in, conv_w, a_log, dt_bias, norm_w, w_out, need_ctx):
    qc, kc, vc, gam_c, bet_c, zc = gdn_project(hc, w_in, conv_w, a_log, dt_bias)
    ql, kl, vl, gam_l, bet_l, zl = gdn_project(hl, w_in, conv_w, a_log, dt_bias)
    s0 = jnp.zeros((hc.shape[0], GDN_HEADS, GDN_DK, GDN_DV), jnp.float32)
    o_c, o_l = [], []
    for d in range(2):
        rev = d == 1
        oc, s_ctx = gated_delta_rule(*[flip(t, rev, 2) for t in (qc, kc, vc, gam_c[d], bet_c[d])], s0)
        ol, _ = gated_delta_rule(*[flip(t, rev, 2) for t in (ql, kl, vl, gam_l[d], bet_l[d])], s_ctx)
        o_c.append(flip(oc, rev, 2))
        o_l.append(flip(ol, rev, 2))
    out_l = gated_head_norm(o_l[0] + o_l[1], zl, norm_w) @ w_out
    out_c = gated_head_norm(o_c[0] + o_c[1], zc, norm_w) @ w_out if need_ctx else None
    return out_c, out_l


def linear_scan(log_a, b, h0):
    def combine(e1, e2):
        la1, b1 = e1
        la2, b2 = e2
        return la1 + la2, jnp.exp(la2) * b1 + b2
    la_cum, h = lax.associative_scan(combine, (log_a, b), axis=1)
    h = h + jnp.exp(la_cum) * h0[:, None, :]
    return h, h[:, -1]


def lru_gates(xr, w_r, b_r, w_i, b_i, lam):
    bsz, L, _ = xr.shape
    xf = xr.astype(jnp.float32)
    xb = xf.reshape(bsz, L, LRU_BLOCKS, LRU_BW)
    r = jax.nn.sigmoid(jnp.einsum('blgi,gij->blgj', xb, w_r.astype(jnp.float32)).reshape(bsz, L, LRU_WIDTH) + b_r)
    ig = jax.nn.sigmoid(jnp.einsum('blgi,gij->blgj', xb, w_i.astype(jnp.float32)).reshape(bsz, L, LRU_WIDTH) + b_i)
    log_a = -LRU_C * r * jax.nn.softplus(-lam.astype(jnp.float32))
    b = jnp.sqrt(-jnp.expm1(2.0 * log_a)) * (ig * xf)
    return log_a, b


def rglru_mixer(hc, hl, w_in, conv_w, conv_b, w_r, b_r, w_i, b_i, lam, w_out, need_ctx):
    def branches(h):
        p = h @ w_in
        return jax.nn.gelu(p[..., :LRU_WIDTH]), short_conv(p[..., LRU_WIDTH:], conv_w, conv_b)
    gate_c, xc = branches(hc)
    gate_l, xl = branches(hl)
    h0 = jnp.zeros((hc.shape[0], LRU_WIDTH), jnp.float32)
    h_c, h_l = [], []
    for d in range(2):
        rev = d == 1
        la, b = lru_gates(xc, w_r[d], b_r[d], w_i[d], b_i[d], lam[d])
        yc, s_ctx = linear_scan(flip(la, rev, 1), flip(b, rev, 1), h0)
        la, b = lru_gates(xl, w_r[d], b_r[d], w_i[d], b_i[d], lam[d])
        yl, _ = linear_scan(flip(la, rev, 1), flip(b, rev, 1), s_ctx)
        h_c.append(flip(yc, rev, 1))
        h_l.append(flip(yl, rev, 1))
    out_l = (gate_l * (h_l[0] + h_l[1]).astype(hl.dtype)) @ w_out
    out_c = (gate_c * (h_c[0] + h_c[1]).astype(hc.dtype)) @ w_out if need_ctx else None
    return out_c, out_l


def gla_chunk(q, k, log_f, v, s0):
    bsz, nh, L, dk = q.shape
    dv = v.shape[-1]
    C = HG_CHUNK
    n = L // C
    q, k, log_f = (t.reshape(bsz, nh, n, C, dk) for t in (q, k, log_f))
    v = v.reshape(bsz, nh, n, C, dv)
    gcum = jnp.cumsum(log_f, axis=-2)
    ref = gcum[..., C // 2:C // 2 + 1, :]
    incl = jnp.tril(jnp.ones((C, C), dtype=bool))
    scores = jnp.einsum('bhnid,bhnjd->bhnij', q * jnp.exp(gcum - ref), k * jnp.exp(ref - gcum))
    o_intra = jnp.where(incl, scores, 0.0) @ v
    glast = gcum[..., -1, :]
    upd = jnp.einsum('bhncd,bhnce->bhnde', k * jnp.exp(glast[..., None, :] - gcum), v)

    def step(s, inp):
        dl, u = inp
        return s * dl[..., :, None] + u, s

    s_fin, s_prev = lax.scan(step, s0, (jnp.moveaxis(jnp.exp(glast), 2, 0), jnp.moveaxis(upd, 2, 0)))
    o_inter = jnp.einsum('bhncd,bhnde->bhnce', q * jnp.exp(gcum), jnp.moveaxis(s_prev, 0, 2))
    return (o_intra + o_inter).reshape(bsz, nh, L, dv), s_fin


def hgrn2_project(h, w_in, lb):
    p = h @ w_in
    q = to_heads(jax.nn.silu(p[..., :HG_QK]), HG_HEADS).astype(jnp.float32)
    v = to_heads(p[..., 3 * HG_QK:3 * HG_QK + HG_V], HG_HEADS).astype(jnp.float32)
    z = p[..., 3 * HG_QK + HG_V:]
    keys, logfs = [], []
    for d in range(2):
        f = lb + (1.0 - lb) * jax.nn.sigmoid(p[..., (1 + d) * HG_QK:(2 + d) * HG_QK].astype(jnp.float32))
        keys.append(to_heads(1.0 - f, HG_HEADS))
        logfs.append(to_heads(jnp.log(f), HG_HEADS))
    return q, v, z, keys, logfs


def hgrn2_mixer(hc, hl, w_in, lb, norm_w, w_out, need_ctx):
    qc, vc, zc, kc, fc = hgrn2_project(hc, w_in, lb)
    ql, vl, zl, kl, fl = hgrn2_project(hl, w_in, lb)
    s0 = jnp.zeros((hc.shape[0], HG_HEADS, HG_DK, HG_DV), jnp.float32)
    o_c, o_l = [], []
    for d in range(2):
        rev = d == 1
        oc, s_ctx = gla_chunk(*[flip(t, rev, 2) for t in (qc, kc[d], fc[d], vc)], s0)
        ol, _ = gla_chunk(*[flip(t, rev, 2) for t in (ql, kl[d], fl[d], vl)], s_ctx)
        o_c.append(flip(oc, rev, 2))
        o_l.append(flip(ol, rev, 2))
    out_l = gated_head_norm(o_l[0] + o_l[1], zl, norm_w) @ w_out
    out_c = gated_head_norm(o_c[0] + o_c[1], zc, norm_w) @ w_out if need_ctx else None
    return out_c, out_l


def setup_inputs(seed: int = 0) -> dict:
    key = jax.random.key(seed)
    keys = iter(jax.random.split(key, 48))
    f32 = jnp.float32

    def normal(shape, scale=1.0):
        return scale * jax.random.normal(next(keys), shape, f32)

    def dense(shape, fan_in, scale=1.0):
        return normal(shape, scale * fan_in ** -0.5)

    def gain(shape):
        return 1.0 + normal(shape, 0.05)

    def uniform(shape, lo, hi):
        return jax.random.uniform(next(keys), shape, f32, lo, hi)

    D = D_MODEL
    dt = jnp.exp(uniform((N_GDN_LAYERS, 2, GDN_HEADS), math.log(1e-3), math.log(1e-1)))
    a0 = uniform((N_LRU_LAYERS, 2, LRU_WIDTH), 0.9, 0.999)
    return {
        'x': normal((BATCH, SEQ, D)),
        'c': normal((BATCH, D)),
        'ctx': normal((BATCH, CTX_LEN, D)),
        'c_ctx': normal((D,)),
        'ada_w': dense((DEPTH, D, 6 * D), D, 0.5),
        'ada_b': normal((DEPTH, 6 * D), 0.01),
        'norm_mix': gain((DEPTH, D)),
        'norm_ffn': gain((DEPTH, D)),
        'norm_final': gain((D,)),
        'ffn_w1': dense((DEPTH, D, FFN_HIDDEN), D),
        'ffn_w3': dense((DEPTH, D, FFN_HIDDEN), D),
        'ffn_w2': dense((DEPTH, FFN_HIDDEN, D), FFN_HIDDEN),
        'gdn_w_in': dense((N_GDN_LAYERS, D, GDN_IN), D),
        'gdn_conv': dense((N_GDN_LAYERS, CONV_W, GDN_QKV), CONV_W),
        'gdn_a_log': jnp.log(uniform((N_GDN_LAYERS, 2, GDN_HEADS), 1.0, 16.0)),
        'gdn_dt_bias': dt + jnp.log(-jnp.expm1(-dt)),
        'gdn_norm': gain((N_GDN_LAYERS, GDN_DV)),
        'gdn_w_out': dense((N_GDN_LAYERS, GDN_Z, D), GDN_Z),
        'lru_w_in': dense((N_LRU_LAYERS, D, 2 * LRU_WIDTH), D),
        'lru_conv_w': dense((N_LRU_LAYERS, CONV_W, LRU_WIDTH), CONV_W),
        'lru_conv_b': normal((N_LRU_LAYERS, LRU_WIDTH), 0.01),
        'lru_w_r': dense((N_LRU_LAYERS, 2, LRU_BLOCKS, LRU_BW, LRU_BW), LRU_BW),
        'lru_b_r': normal((N_LRU_LAYERS, 2, LRU_WIDTH), 0.01),
        'lru_w_i': dense((N_LRU_LAYERS, 2, LRU_BLOCKS, LRU_BW, LRU_BW), LRU_BW),
        'lru_b_i': normal((N_LRU_LAYERS, 2, LRU_WIDTH), 0.01),
        'lru_lambda': jnp.log(a0) - jnp.log1p(-a0),
        'lru_w_out': dense((N_LRU_LAYERS, LRU_WIDTH, D), LRU_WIDTH),
        'hg_w_in': dense((N_HG_LAYERS, D, HG_IN), D),
        'hg_lb_logits': normal((DEPTH, HG_QK), 0.1),
        'hg_norm': gain((N_HG_LAYERS, HG_DV)),
        'hg_w_out': dense((N_HG_LAYERS, HG_V, D), HG_V),
    }


def reference(x, c, ctx, c_ctx, ada_w, ada_b, norm_mix, norm_ffn, norm_final,
              ffn_w1, ffn_w3, ffn_w2,
              gdn_w_in, gdn_conv, gdn_a_log, gdn_dt_bias, gdn_norm, gdn_w_out,
              lru_w_in, lru_conv_w, lru_conv_b, lru_w_r, lru_b_r, lru_w_i, lru_b_i,
              lru_lambda, lru_w_out,
              hg_w_in, hg_lb_logits, hg_norm, hg_w_out):
    lb_p = jax.nn.softmax(hg_lb_logits.astype(jnp.float32), axis=0)
    lower_bounds = jnp.cumsum(lb_p, axis=0) - lb_p[0]
    silu_c = jax.nn.silu(c)[:, None, :]
    silu_cc = jax.nn.silu(c_ctx)[None, None, :]
    xl, xc = x, ctx
    for i in range(DEPTH):
        last = i == DEPTH - 1
        mod_l = jnp.split(silu_c @ ada_w[i] + ada_b[i], 6, axis=-1)
        mod_c = jnp.split(silu_cc @ ada_w[i] + ada_b[i], 6, axis=-1)
        col = i % 2 == 1
        hl = to_scan_order(modulate(rmsnorm(xl, norm_mix[i]), mod_l[0], mod_l[1]), col)
        hc = modulate(rmsnorm(xc, norm_mix[i]), mod_c[0], mod_c[1])
        kind, j = i % N_MIXERS, i // N_MIXERS
        if kind == 0:
            oc, ol = gdn_mixer(hc, hl, gdn_w_in[j], gdn_conv[j], gdn_a_log[j], gdn_dt_bias[j],
                               gdn_norm[j], gdn_w_out[j], not last)
        elif kind == 1:
            oc, ol = rglru_mixer(hc, hl, lru_w_in[j], lru_conv_w[j], lru_conv_b[j], lru_w_r[j],
                                 lru_b_r[j], lru_w_i[j], lru_b_i[j], lru_lambda[j], lru_w_out[j],
                                 not last)
        else:
            oc, ol = hgrn2_mixer(hc, hl, hg_w_in[j], lower_bounds[i], hg_norm[j], hg_w_out[j],
                                 not last)
        xl = xl + mod_l[2] * from_scan_order(ol, col)
        xl = xl + mod_l[5] * swiglu(modulate(rmsnorm(xl, norm_ffn[i]), mod_l[3], mod_l[4]),
                                    ffn_w1[i], ffn_w3[i], ffn_w2[i])
        if not last:
            xc = xc + mod_c[2] * oc
            xc = xc + mod_c[5] * swiglu(modulate(rmsnorm(xc, norm_ffn[i]), mod_c[3], mod_c[4]),
                                        ffn_w1[i], ffn_w3[i], ffn_w2[i])
    return rmsnorm(xl, norm_final)
```

```python
import functools
import math

import jax
import jax.numpy as jnp
from jax import lax
from jax.experimental import pallas as pl
from jax.experimental.pallas import tpu as pltpu

F32 = jnp.float32
BF16 = jnp.bfloat16

D_MODEL = 1024
HEADS = 8
DH = 128
CHUNK = 64
GRID_W = 64
EPS = 1e-6
LRU_C = 8.0
FFN_HIDDEN = 2816
FFN_COLS = 256
TOKEN_TILE = 512
V7X_VMEM_BYTES = 64 * 1024 * 1024
VMEM_LIMIT = V7X_VMEM_BYTES - 6 * 1024 * 1024


def _cparams(*sem):
    return pltpu.CompilerParams(dimension_semantics=sem, vmem_limit_bytes=VMEM_LIMIT)


def _mm(a, b):
    return jnp.dot(a, b, preferred_element_type=F32)


def _mm_nt(a, b):
    return lax.dot_general(a, b, (((1,), (1,)), ((), ())), preferred_element_type=F32)


def _mm_tn(a, b):
    return lax.dot_general(a, b, (((0,), (0,)), ((), ())), preferred_element_type=F32)


def _silu(x):
    return x * jax.nn.sigmoid(x)


def _softplus(x):
    return jnp.maximum(x, 0.0) + jnp.log(1.0 + jnp.exp(-jnp.abs(x)))


def _gelu_tanh(x):
    return 0.5 * x * (1.0 + jnp.tanh(math.sqrt(2.0 / math.pi) * (x + 0.044715 * (x * x * x))))


def _norm_mod(x, nw, shift, scale):
    ms = jnp.mean(x * x, axis=-1, keepdims=True)
    y = x * lax.rsqrt(ms + EPS) * nw
    return y * (1.0 + scale) + shift


def _shift_rows(x, k):
    n = x.shape[0]
    return x if k % n == 0 else pltpu.roll(x, (-k) % n, 0)


def _chunk_cumsum(x, rows_in_chunk, rev):
    k = 1
    while k < CHUNK:
        if rev:
            x = x + jnp.where(rows_in_chunk < CHUNK - k, _shift_rows(x, k), 0.0)
        else:
            x = x + jnp.where(rows_in_chunk >= k, _shift_rows(x, -k), 0.0)
        k *= 2
    return x


def _conv_chunk(ref, t, n_chunks, cw):
    first = isinstance(t, int) and t == 0
    last = isinstance(t, int) and t == n_chunks - 1
    halo = 16
    if first:
        win = ref[0, 0:CHUNK + halo, :].astype(F32)
        off = 0
    elif last:
        start = t * CHUNK - halo
        win = ref[0, start:start + CHUNK + halo, :].astype(F32)
        off = halo
    else:
        start = pl.multiple_of(t * CHUNK - halo, halo)
        win = ref[0, pl.ds(start, CHUNK + 2 * halo), :].astype(F32)
        off = halo
    rows = lax.broadcasted_iota(jnp.int32, (CHUNK, win.shape[1]), 0)

    def tap(k):
        return _shift_rows(win, k)[off:off + CHUNK]

    xm2, xm1, x0, xp1 = tap(-2), tap(-1), tap(0), tap(1)
    if first:
        xm2 = jnp.where(rows >= 2, xm2, 0.0)
        xm1 = jnp.where(rows >= 1, xm1, 0.0)
    if last:
        xp1 = jnp.where(rows < CHUNK - 1, xp1, 0.0)
    return cw[0:1] * xm2 + cw[1:2] * xm1 + cw[2:3] * x0 + cw[3:4] * xp1


def _for_chunks(n_chunks, fn):
    fn(0)
    if n_chunks > 2:
        def body(t, carry):
            fn(t)
            return carry
        lax.fori_loop(1, n_chunks - 1, body, 0)
    if n_chunks > 1:
        fn(n_chunks - 1)


def _rows(t):
    return pl.ds(pl.multiple_of(t * CHUNK, CHUNK), CHUNK)


def _gated_norm_out(of, ob, z_ref, y_ref, nw, base, n_chunks):
    def body(t, carry):
        r = _rows(t)
        rb = pl.ds(pl.multiple_of(base + t * CHUNK, CHUNK), CHUNK)
        o = of[rb, :] + ob[rb, :]
        ms = jnp.mean(o * o, axis=-1, keepdims=True)
        y = o * lax.rsqrt(ms + EPS) * nw
        y_ref[0, r, :] = (y * _silu(z_ref[0, r, :].astype(F32))).astype(y_ref.dtype)
        return carry
    lax.fori_loop(0, n_chunks, body, 0)


def _mods_kernel(cv_ref, w_ref, b_ref, o_ref):
    s = _silu(cv_ref[...])
    o_ref[0] = jnp.dot(s, w_ref[0], preferred_element_type=F32,
                       precision=lax.Precision.HIGHEST) + b_ref[0]


def _mods(cvec, ada_w, ada_b):
    depth, d, n = ada_w.shape
    tn = 1536
    return pl.pallas_call(
        _mods_kernel,
        grid=(depth, n // tn),
        in_specs=[
            pl.BlockSpec((8, d), lambda i, j: (0, 0)),
            pl.BlockSpec((1, d, tn), lambda i, j: (i, 0, j)),
            pl.BlockSpec((1, 1, tn), lambda i, j: (i, 0, j)),
        ],
        out_specs=pl.BlockSpec((1, 8, tn), lambda i, j: (i, 0, j)),
        out_shape=jax.ShapeDtypeStruct((depth, 8, n), F32),
        compiler_params=_cparams("parallel", "parallel"),
        name="adaln_mods",
    )(cvec, ada_w, ada_b.reshape(depth, 1, n))


def _token_tile(L):
    return min(TOKEN_TILE, L)


def _proj_in_specs(tm, d):
    return [
        pl.BlockSpec((1, tm, d), lambda b, t: (b, t, 0)),
        pl.BlockSpec((1, d), lambda b, t: (0, 0)),
        pl.BlockSpec((1, 6, d), lambda b, t: (b, 0, 0)),
    ]


def _full(shape):
    nd = len(shape)
    return pl.BlockSpec(shape, lambda b, t: (0,) * nd)


def _gdn_proj_kernel(x_ref, nw_ref, mod_ref, w_ref, wabh_ref, wabl_ref, alog_ref, dtb_ref,
                     qkvz_ref, gcol_ref, grow_ref):
    h = _norm_mod(x_ref[0], nw_ref[...], mod_ref[0, 0:1], mod_ref[0, 1:2])
    hb = h.astype(BF16)
    n = w_ref.shape[1]
    for c in range(n // D_MODEL):
        cs = slice(c * D_MODEL, (c + 1) * D_MODEL)
        qkvz_ref[0, :, cs] = _mm(hb, w_ref[:, cs]).astype(qkvz_ref.dtype)
    hl = (h - hb.astype(F32)).astype(BF16)
    ab = _mm(hb, wabh_ref[...]) + _mm(hl, wabh_ref[...]) + _mm(hb, wabl_ref[...])
    tm = ab.shape[0]
    lane = lax.broadcasted_iota(jnp.int32, ab.shape, 1)
    r64 = lax.broadcasted_iota(jnp.int32, ab.shape, 0) & (CHUNK - 1)
    g = -jnp.exp(alog_ref[...]) * _softplus(ab + dtb_ref[...])
    beta = jax.nn.sigmoid(ab)
    pre = _chunk_cumsum(g, r64, False)
    suf = _chunk_cumsum(g, r64, True)
    res = jnp.where(lane < HEADS, pre,
                    jnp.where(lane < 2 * HEADS, suf,
                              jnp.where(lane < 4 * HEADS, beta, pre + suf - g)))
    gcol_ref[0] = res
    grow_ref[0] = res.T[0:4 * HEADS, :]


def _gdn_proj(x, nw, mods, w_main, wab_hi, wab_lo, alog, dtb):
    B, L, d = x.shape
    tm = _token_tile(L)
    n = w_main.shape[1]
    return pl.pallas_call(
        _gdn_proj_kernel,
        grid=(B, L // tm),
        in_specs=_proj_in_specs(tm, d) + [
            _full(w_main.shape), _full(wab_hi.shape), _full(wab_lo.shape),
            _full(alog.shape), _full(dtb.shape)],
        out_specs=[
            pl.BlockSpec((1, tm, n), lambda b, t: (b, t, 0)),
            pl.BlockSpec((1, tm, DH), lambda b, t: (b, t, 0)),
            pl.BlockSpec((1, 4 * HEADS, tm), lambda b, t: (b, 0, t)),
        ],
        out_shape=[
            jax.ShapeDtypeStruct((B, L, n), BF16),
            jax.ShapeDtypeStruct((B, L, DH), F32),
            jax.ShapeDtypeStruct((B, 4 * HEADS, L), F32),
        ],
        compiler_params=_cparams("parallel", "parallel"),
        name="gdn_proj",
    )(x, nw, mods, w_main, wab_hi, wab_lo, alog, dtb)


def _lru_proj_kernel(x_ref, nw_ref, mod_ref, w_ref, gate_ref, xr_ref):
    h = _norm_mod(x_ref[0], nw_ref[...], mod_ref[0, 0:1], mod_ref[0, 1:2])
    hb = h.astype(BF16)
    gate_ref[0] = _gelu_tanh(_mm(hb, w_ref[:, 0:D_MODEL])).astype(gate_ref.dtype)
    xr_ref[0] = _mm(hb, w_ref[:, D_MODEL:2 * D_MODEL])


def _lru_proj(x, nw, mods, w):
    B, L, d = x.shape
    tm = _token_tile(L)
    spec = pl.BlockSpec((1, tm, d), lambda b, t: (b, t, 0))
    return pl.pallas_call(
        _lru_proj_kernel,
        grid=(B, L // tm),
        in_specs=_proj_in_specs(tm, d) + [_full(w.shape)],
        out_specs=[spec, spec],
        out_shape=[jax.ShapeDtypeStruct((B, L, d), BF16), jax.ShapeDtypeStruct((B, L, d), F32)],
        compiler_params=_cparams("parallel", "parallel"),
        name="lru_proj",
    )(x, nw, mods, w)


def _hg_proj_kernel(x_ref, nw_ref, mod_ref, w_ref, lbl_ref, q_ref, f0_ref, f1_ref, v_ref, z_ref,
                    *, layer):
    logits = lbl_ref[...]
    depth = logits.shape[0]
    m = logits[0:1]
    for r in range(1, depth):
        m = jnp.maximum(m, logits[r:r + 1])
    e = jnp.exp(logits - m)
    tot = e[0:1]
    for r in range(1, depth):
        tot = tot + e[r:r + 1]
    lb = jnp.zeros_like(tot)
    for r in range(1, layer + 1):
        lb = lb + e[r:r + 1] / tot

    h = _norm_mod(x_ref[0], nw_ref[...], mod_ref[0, 0:1], mod_ref[0, 1:2])
    hb = h.astype(BF16)

    def cols(c):
        return _mm(hb, w_ref[:, c * D_MODEL:(c + 1) * D_MODEL])

    q_ref[0] = _silu(cols(0)).astype(q_ref.dtype)
    for d, f_ref in enumerate((f0_ref, f1_ref)):
        f = lb + (1.0 - lb) * jax.nn.sigmoid(cols(1 + d))
        f_ref[0] = jnp.log(f)
    v_ref[0] = cols(3).astype(v_ref.dtype)
    z_ref[0] = cols(4).astype(z_ref.dtype)


def _hg_proj(x, nw, mods, w, lb_logits, layer):
    B, L, d = x.shape
    tm = _token_tile(L)
    spec = pl.BlockSpec((1, tm, d), lambda b, t: (b, t, 0))
    sds = lambda dt: jax.ShapeDtypeStruct((B, L, d), dt)
    return pl.pallas_call(
        functools.partial(_hg_proj_kernel, layer=layer),
        grid=(B, L // tm),
        in_specs=_proj_in_specs(tm, d) + [_full(w.shape), _full(lb_logits.shape)],
        out_specs=[spec] * 5,
        out_shape=[sds(BF16), sds(F32), sds(F32), sds(BF16), sds(BF16)],
        compiler_params=_cparams("parallel", "parallel"),
        name="hg_proj",
    )(x, nw, mods, w, lb_logits)


def _post_kernel(x_ref, y_ref, mod_ref, nw_ref, wo_ref, w1_ref, w3_ref, w2_ref, o_ref, acc_ref):
    x1 = x_ref[0] + mod_ref[0, 2:3] * _mm(y_ref[0], wo_ref[...])
    hb = _norm_mod(x1, nw_ref[...], mod_ref[0, 3:4], mod_ref[0, 4:5]).astype(BF16)
    hidden = w1_ref.shape[1]
    for j in range(hidden // FFN_COLS):
        cs = slice(j * FFN_COLS, (j + 1) * FFN_COLS)
        a = (_silu(_mm(hb, w1_ref[:, cs])) * _mm(hb, w3_ref[:, cs])).astype(BF16)
        part = _mm(a, w2_ref[cs, :])
        if j == 0:
            acc_ref[...] = part
        else:
            acc_ref[...] += part
    o_ref[0] = x1 + mod_ref[0, 5:6] * acc_ref[...]


def _post(x, y, mods, nw, wo, w1, w3, w2):
    B, L, d = x.shape
    tm = _token_tile(L)
    tile = pl.BlockSpec((1, tm, d), lambda b, t: (b, t, 0))

    def resident(shape):
        return pl.BlockSpec(shape, lambda b, t: (0, 0), pipeline_mode=pl.Buffered(1))

    return pl.pallas_call(
        _post_kernel,
        grid=(B, L // tm),
        in_specs=[tile, tile,
                  pl.BlockSpec((1, 6, d), lambda b, t: (b, 0, 0)),
                  pl.BlockSpec((1, d), lambda b, t: (0, 0)),
                  resident(wo.shape), resident(w1.shape), resident(w3.shape), resident(w2.shape)],
        out_specs=tile,
        out_shape=jax.ShapeDtypeStruct((B, L, d), F32),
        scratch_shapes=[pltpu.VMEM((tm, d), F32)],
        compiler_params=_cparams("parallel", "parallel"),
        name="post_ffn",
    )(x, y, mods, nw, wo, w1, w3, w2)


def _final_norm_kernel(x_ref, nw_ref, o_ref):
    x = x_ref[0]
    ms = jnp.mean(x * x, axis=-1, keepdims=True)
    o_ref[0] = x * lax.rsqrt(ms + EPS) * nw_ref[...]


def _final_norm(x, nw):
    B, L, d = x.shape
    tm = _token_tile(L)
    tile = pl.BlockSpec((1, tm, d), lambda b, t: (b, t, 0))
    return pl.pallas_call(
        _final_norm_kernel,
        grid=(B, L // tm),
        in_specs=[tile, pl.BlockSpec((1, d), lambda b, t: (0, 0))],
        out_specs=tile,
        out_shape=jax.ShapeDtypeStruct((B, L, d), F32),
        compiler_params=_cparams("parallel", "parallel"),
        name="final_norm",
    )(x, nw)


def _gdn_prep(q_ref, k_ref, v_ref, cwq, cwk, cwv, qs, ks, vs, base, n_chunks):
    def fn(t):
        dst = pl.ds(base + t * CHUNK, CHUNK) if isinstance(t, int) else \
            pl.ds(pl.multiple_of(base + t * CHUNK, CHUNK), CHUNK)
        q = _silu(_conv_chunk(q_ref, t, n_chunks, cwq))
        q = q * lax.rsqrt(jnp.sum(q * q, axis=-1, keepdims=True) + EPS) * (DH ** -0.5)
        k = _silu(_conv_chunk(k_ref, t, n_chunks, cwk))
        k = k * lax.rsqrt(jnp.sum(k * k, axis=-1, keepdims=True) + EPS)
        v = _silu(_conv_chunk(v_ref, t, n_chunks, cwv))
        qs[dst, :] = q.astype(qs.dtype)
        ks[dst, :] = k.astype(ks.dtype)
        vs[dst, :] = v.astype(vs.dtype)
    _for_chunks(n_chunks, fn)


def _gdn_chunk(q16, k16, v16, gcr, g_row, b_row, s_ref, rev):
    o = HEADS if rev else 0
    g_col = gcr[:, o:o + 1]
    b_col = gcr[:, 2 * HEADS + o:2 * HEADS + o + 1]
    gl_col = gcr[:, 4 * HEADS + o:4 * HEADS + o + 1]
    ii = lax.broadcasted_iota(jnp.int32, (CHUNK, CHUNK), 0)
    jj = lax.broadcasted_iota(jnp.int32, (CHUNK, CHUNK), 1)
    incl = (ii <= jj) if rev else (ii >= jj)
    strict = (ii < jj) if rev else (ii > jj)
    decay = jnp.where(incl, jnp.exp(jnp.where(incl, g_col - g_row, 0.0)), 0.0)
    kq = _mm_nt(jnp.concatenate([k16, q16], axis=0), k16)
    kk = kq[0:CHUNK]
    qk = kq[CHUNK:2 * CHUNK]
    nmat = jnp.where(strict, -(kk * b_col * decay), 0.0)
    tinv = jnp.where(ii == jj, 1.0, 0.0) + nmat
    power = nmat
    for _ in range(5):
        power = jnp.dot(power, power, preferred_element_type=F32, precision=lax.Precision.HIGHEST)
        tinv = tinv + jnp.dot(power, tinv, preferred_element_type=F32,
                              precision=lax.Precision.HIGHEST)
    tu = (tinv * b_row).astype(BF16)
    tw = (tinv * (b_row * jnp.exp(g_row))).astype(BF16)
    u = _mm(tu, v16)
    w = _mm(tw, k16)
    aqk = jnp.where(incl, qk * decay, 0.0).astype(BF16)
    s = s_ref[...]
    s16 = s.astype(BF16)
    wq = _mm(jnp.concatenate([w.astype(BF16), q16], axis=0), s16)
    v_new = u - wq[0:CHUNK]
    out = jnp.exp(g_col) * wq[CHUNK:2 * CHUNK] + _mm(aqk, v_new.astype(BF16))
    vd = (jnp.exp(gl_col - g_col) * v_new).astype(BF16)
    s_ref[...] = s * jnp.exp(gl_col[0:1, :]) + _mm_tn(k16, vd)
    return out


def _gdn_mixer_kernel(qc, kc, vc, zc, gcc, grc, ql, kl, vl, zl, gcl, grl,
                      cwq, cwk, cwv, nw, yc, yl, qs, ks, vs, of, ob, sf, sb, *, Lc, Ll):
    head = pl.program_id(1)
    nc, nl = Lc // CHUNK, Ll // CHUNK
    cwq_v, cwk_v, cwv_v = cwq[...], cwk[...], cwv[...]
    _gdn_prep(qc, kc, vc, cwq_v, cwk_v, cwv_v, qs, ks, vs, 0, nc)
    _gdn_prep(ql, kl, vl, cwq_v, cwk_v, cwv_v, qs, ks, vs, Lc, nl)
    sf[...] = jnp.zeros_like(sf)
    sb[...] = jnp.zeros_like(sb)
    lane_shift = (DH - head) % DH

    def scan(gc_ref, gr_ref, base, n):
        def one(t, s_ref, o_ref, rev):
            rb = pl.ds(pl.multiple_of(base + t * CHUNK, CHUNK), CHUNK)
            gcr = pltpu.roll(gc_ref[0, _rows(t), :], lane_shift, 1)
            row = head + (HEADS if rev else 0)
            g_row = gr_ref[0, row, pl.ds(t, 1), :]
            b_row = gr_ref[0, row + 2 * HEADS, pl.ds(t, 1), :]
            o_ref[rb, :] = _gdn_chunk(qs[rb, :], ks[rb, :], vs[rb, :], gcr, g_row, b_row, s_ref, rev)

        def body(i, carry):
            one(i, sf, of, False)
            one(n - 1 - i, sb, ob, True)
            return carry
        lax.fori_loop(0, n, body, 0)

    scan(gcc, grc, 0, nc)
    scan(gcl, grl, Lc, nl)
    nw_v = nw[...]
    _gated_norm_out(of, ob, zc, yc, nw_v, 0, nc)
    _gated_norm_out(of, ob, zl, yl, nw_v, Lc, nl)


def _gdn_mixer(pc, gcc, grc, plat, gcl, grl, conv_w, norm_w):
    B, Lc, _ = pc.shape
    Ll = plat.shape[1]
    nc, nl = Lc // CHUNK, Ll // CHUNK
    grc = grc.reshape(B, 4 * HEADS, nc, CHUNK)
    grl = grl.reshape(B, 4 * HEADS, nl, CHUNK)

    def seq_specs(L, n):
        head_blk = lambda k: pl.BlockSpec((1, L, DH), lambda b, h, k=k: (b, 0, k * HEADS + h))
        return [head_blk(0), head_blk(1), head_blk(2), head_blk(3),
                pl.BlockSpec((1, L, DH), lambda b, h: (b, 0, 0)),
                pl.BlockSpec((1, 4 * HEADS, n, CHUNK), lambda b, h: (b, 0, 0, 0))]

    cw_blk = lambda k: pl.BlockSpec((4, DH), lambda b, h, k=k: (0, k * HEADS + h))
    y_spec = lambda L: pl.BlockSpec((1, L, DH), lambda b, h: (b, 0, h))
    Lt = Lc + Ll
    return pl.pallas_call(
        functools.partial(_gdn_mixer_kernel, Lc=Lc, Ll=Ll),
        grid=(B, HEADS),
        in_specs=seq_specs(Lc, nc) + seq_specs(Ll, nl) + [
            cw_blk(0), cw_blk(1), cw_blk(2), pl.BlockSpec((1, DH), lambda b, h: (0, 0))],
        out_specs=[y_spec(Lc), y_spec(Ll)],
        out_shape=[jax.ShapeDtypeStruct((B, Lc, HEADS * DH), BF16),
                   jax.ShapeDtypeStruct((B, Ll, HEADS * DH), BF16)],
        scratch_shapes=[pltpu.VMEM((Lt, DH), BF16)] * 3 + [pltpu.VMEM((Lt, DH), F32)] * 2
        + [pltpu.VMEM((DH, DH), F32)] * 2,
        compiler_params=_cparams("parallel", "arbitrary"),
        name="gdn_mixer",
    )(pc, pc, pc, pc, gcc, grc, plat, plat, plat, plat, gcl, grl,
      conv_w, conv_w, conv_w, norm_w)


def _hg_chunk(q16, logf, v16, st_ref, rev):
    rows = lax.broadcasted_iota(jnp.int32, (CHUNK, DH), 0)
    gcum = _chunk_cumsum(logf, rows, rev)
    mid = CHUNK // 2 - 1 if rev else CHUNK // 2
    end = 0 if rev else CHUNK - 1
    ref = gcum[mid:mid + 1, :]
    glast = gcum[end:end + 1, :]
    q = q16.astype(F32)
    k = 1.0 - jnp.exp(logf)
    ii = lax.broadcasted_iota(jnp.int32, (CHUNK, CHUNK), 0)
    jj = lax.broadcasted_iota(jnp.int32, (CHUNK, CHUNK), 1)
    incl = (ii <= jj) if rev else (ii >= jj)
    scores = _mm_nt((q * jnp.exp(gcum - ref)).astype(BF16), (k * jnp.exp(ref - gcum)).astype(BF16))
    scores = jnp.where(incl, scores, 0.0).astype(BF16)
    st = st_ref[...]
    out = _mm(scores, v16) + _mm_nt((q * jnp.exp(gcum)).astype(BF16), st.astype(BF16))
    kd = (k * jnp.exp(glast - gcum)).astype(BF16)
    st_ref[...] = st * jnp.exp(glast) + _mm_tn(v16, kd)
    return out


def _hg_mixer_kernel(qc, f0c, f1c, vc, zc, ql, f0l, f1l, vl, zl, nw, yc, yl, of, ob, sf, sb,
                     *, Lc, Ll):
    nc, nl = Lc // CHUNK, Ll // CHUNK
    sf[...] = jnp.zeros_like(sf)
    sb[...] = jnp.zeros_like(sb)

    def scan(q_ref, f0_ref, f1_ref, v_ref, base, n):
        def one(t, f_ref, s_ref, o_ref, rev):
            r = _rows(t)
            rb = pl.ds(pl.multiple_of(base + t * CHUNK, CHUNK), CHUNK)
            o_ref[rb, :] = _hg_chunk(q_ref[0, r, :], f_ref[0, r, :], v_ref[0, r, :], s_ref, rev)

        def body(i, carry):
            one(i, f0_ref, sf, of, False)
            one(n - 1 - i, f1_ref, sb, ob, True)
            return carry
        lax.fori_loop(0, n, body, 0)

    scan(qc, f0c, f1c, vc, 0, nc)
    scan(ql, f0l, f1l, vl, Lc, nl)
    nw_v = nw[...]
    _gated_norm_out(of, ob, zc, yc, nw_v, 0, nc)
    _gated_norm_out(of, ob, zl, yl, nw_v, Lc, nl)


def _hg_mixer(ctx_parts, lat_parts, norm_w):
    B, Lc, _ = ctx_parts[0].shape
    Ll = lat_parts[0].shape[1]
    blk = lambda L: pl.BlockSpec((1, L, DH), lambda b, h: (b, 0, h))
    Lt = Lc + Ll
    return pl.pallas_call(
        functools.partial(_hg_mixer_kernel, Lc=Lc, Ll=Ll),
        grid=(B, HEADS),
        in_specs=[blk(Lc)] * 5 + [blk(Ll)] * 5 + [pl.BlockSpec((1, DH), lambda b, h: (0, 0))],
        out_specs=[blk(Lc), blk(Ll)],
        out_shape=[jax.ShapeDtypeStruct((B, Lc, HEADS * DH), BF16),
                   jax.ShapeDtypeStruct((B, Ll, HEADS * DH), BF16)],
        scratch_shapes=[pltpu.VMEM((Lt, DH), F32)] * 2 + [pltpu.VMEM((DH, DH), F32)] * 2,
        compiler_params=_cparams("parallel", "arbitrary"),
        name="hg_mixer",
    )(*ctx_parts, *lat_parts, norm_w)


def _lru_chunk(x, w_ref, b_ref, sp, h0, d):
    rev = d == 1
    cs = slice(2 * DH * d, 2 * DH * (d + 1))
    gates = _mm(x.astype(BF16), w_ref[0, :, cs]) + b_ref[0, :, cs]
    r = jax.nn.sigmoid(gates[:, 0:DH])
    ig = jax.nn.sigmoid(gates[:, DH:2 * DH])
    log_a = -LRU_C * r * sp
    a = jnp.exp(log_a)
    bv = jnp.sqrt(1.0 - jnp.exp(2.0 * log_a)) * (ig * x)
    rows = lax.broadcasted_iota(jnp.int32, (CHUNK, DH), 0)
    k = 1
    while k < CHUNK:
        if rev:
            m = rows < CHUNK - k
            a_sh, b_sh = _shift_rows(a, k), _shift_rows(bv, k)
        else:
            m = rows >= k
            a_sh, b_sh = _shift_rows(a, -k), _shift_rows(bv, -k)
        bv = jnp.where(m, bv + a * b_sh, bv)
        a = jnp.where(m, a * a_sh, a)
        k *= 2
    hrows = bv + a * h0
    end = 0 if rev else CHUNK - 1
    return hrows, hrows[end:end + 1, :]


def _lru_mixer_kernel(xc, gc, xl, gl, cw, cb, w4, b4, lam, yc, yl, xs, of, ob, *, Lc, Ll):
    nc, nl = Lc // CHUNK, Ll // CHUNK
    cw_v = cw[...]
    cb_v = cb[...]

    def prep(x_ref, base, n):
        def fn(t):
            dst = pl.ds(base + t * CHUNK, CHUNK) if isinstance(t, int) else \
                pl.ds(pl.multiple_of(base + t * CHUNK, CHUNK), CHUNK)
            xs[dst, :] = _conv_chunk(x_ref, t, n, cw_v) + cb_v
        _for_chunks(n, fn)

    prep(xc, 0, nc)
    prep(xl, Lc, nl)
    sp0 = _softplus(-lam[0:1, :])
    sp1 = _softplus(-lam[1:2, :])

    def scan(base, n, carry):
        def body(i, c):
            hf, hb = c
            rf = pl.ds(pl.multiple_of(base + i * CHUNK, CHUNK), CHUNK)
            rbk = pl.ds(pl.multiple_of(base + (n - 1 - i) * CHUNK, CHUNK), CHUNK)
            yf, hf = _lru_chunk(xs[rf, :], w4, b4, sp0, hf, 0)
            of[rf, :] = yf
            yb, hb = _lru_chunk(xs[rbk, :], w4, b4, sp1, hb, 1)
            ob[rbk, :] = yb
            return hf, hb
        return lax.fori_loop(0, n, body, carry)

    zero = jnp.zeros((1, DH), F32)
    carry = scan(0, nc, (zero, zero))
    scan(Lc, nl, carry)

    def out(g_ref, y_ref, base, n):
        def body(t, c):
            r = _rows(t)
            rb = pl.ds(pl.multiple_of(base + t * CHUNK, CHUNK), CHUNK)
            y_ref[0, r, :] = (g_ref[0, r, :].astype(F32) * (of[rb, :] + ob[rb, :])).astype(y_ref.dtype)
            return c
        lax.fori_loop(0, n, body, 0)

    out(gc, yc, 0, nc)
    out(gl, yl, Lc, nl)


def _lru_mixer(xr_c, gate_c, xr_l, gate_l, conv_w, conv_b, w4, b4, lam):
    B, Lc, _ = xr_c.shape
    Ll = xr_l.shape[1]
    blk = lambda L: pl.BlockSpec((1, L, DH), lambda b, g: (b, 0, g))
    Lt = Lc + Ll
    return pl.pallas_call(
        functools.partial(_lru_mixer_kernel, Lc=Lc, Ll=Ll),
        grid=(B, HEADS),
        in_specs=[blk(Lc), blk(Lc), blk(Ll), blk(Ll),
                  pl.BlockSpec((4, DH), lambda b, g: (0, g)),
                  pl.BlockSpec((1, DH), lambda b, g: (0, g)),
                  pl.BlockSpec((1, DH, 4 * DH), lambda b, g: (g, 0, 0)),
                  pl.BlockSpec((1, 1, 4 * DH), lambda b, g: (g, 0, 0)),
                  pl.BlockSpec((2, DH), lambda b, g: (0, g))],
        out_specs=[blk(Lc), blk(Ll)],
        out_shape=[jax.ShapeDtypeStruct((B, Lc, HEADS * DH), BF16),
                   jax.ShapeDtypeStruct((B, Ll, HEADS * DH), BF16)],
        scratch_shapes=[pltpu.VMEM((Lt, DH), F32)] * 3,
        compiler_params=_cparams("parallel", "arbitrary"),
        name="lru_mixer",
    )(xr_c, gate_c, xr_l, gate_l, conv_w, conv_b, w4, b4, lam)


def _to_scan_order(h):
    bsz, L, d = h.shape
    return h.reshape(bsz, L // GRID_W, GRID_W, d).transpose(0, 2, 1, 3).reshape(bsz, L, d)


def _from_scan_order(h):
    bsz, L, d = h.shape
    return h.reshape(bsz, GRID_W, L // GRID_W, d).transpose(0, 2, 1, 3).reshape(bsz, L, d)


def _pad_lanes(v, layout):
    row = jnp.zeros((DH,), F32)
    for off in layout:
        row = row.at[off:off + v.shape[0]].set(v)
    return row.reshape(1, DH)


def kernel(x, c, ctx, c_ctx, ada_w, ada_b, norm_mix, norm_ffn, norm_final, ffn_w1, ffn_w3, ffn_w2, gdn_w_in, gdn_conv, gdn_a_log, gdn_dt_bias, gdn_norm, gdn_w_out, lru_w_in, lru_conv_w, lru_conv_b, lru_w_r, lru_b_r, lru_w_i, lru_b_i, lru_lambda, lru_w_out, hg_w_in, hg_lb_logits, hg_norm, hg_w_out):
    B, L, d = x.shape
    depth = ada_w.shape[0]
    assert d == D_MODEL and L % GRID_W == 0 and L % CHUNK == 0 and ctx.shape[1] % CHUNK == 0
    assert B + 1 <= 8

    cvec = jnp.zeros((8, d), F32).at[0:B].set(c).at[B].set(c_ctx)
    mods = _mods(cvec, ada_w, ada_b).reshape(depth, 8, 6, d)

    xl, xc = x, ctx
    for i in range(depth):
        mod_l = mods[i, 0:B]
        mod_c = jnp.broadcast_to(mods[i, B][None], (B, 6, d))
        col = i % 2 == 1
        if col:
            xl = _to_scan_order(xl)
        nw = norm_mix[i].reshape(1, d)
        kind, j = i % 3, i // 3
        if kind == 0:
            w_in = gdn_w_in[j]
            n_main = 4 * HEADS * DH
            w_main = w_in[:, :n_main].astype(BF16)
            wa = w_in[:, n_main:n_main + 2 * HEADS]
            wb = w_in[:, n_main + 2 * HEADS:n_main + 4 * HEADS]
            wab = jnp.zeros((d, DH), F32)
            wab = wab.at[:, 0:2 * HEADS].set(wa).at[:, 2 * HEADS:4 * HEADS].set(wb)
            wab = wab.at[:, 4 * HEADS:6 * HEADS].set(wa)
            wab_hi = wab.astype(BF16)
            wab_lo = (wab - wab_hi.astype(F32)).astype(BF16)
            alog = _pad_lanes(gdn_a_log[j].reshape(-1), (0, 4 * HEADS))
            dtb = _pad_lanes(gdn_dt_bias[j].reshape(-1), (0, 4 * HEADS))
            pc, gcc, grc = _gdn_proj(xc, nw, mod_c, w_main, wab_hi, wab_lo, alog, dtb)
            plat, gcl, grl = _gdn_proj(xl, nw, mod_l, w_main, wab_hi, wab_lo, alog, dtb)
            yc, yl = _gdn_mixer(pc, gcc, grc, plat, gcl, grl, gdn_conv[j],
                                gdn_norm[j].reshape(1, DH))
            w_out = gdn_w_out[j]
        elif kind == 1:
            w_in = lru_w_in[j].astype(BF16)
            gate_c, xr_c = _lru_proj(xc, nw, mod_c, w_in)
            gate_l, xr_l = _lru_proj(xl, nw, mod_l, w_in)
            w4 = jnp.concatenate([lru_w_r[j, 0], lru_w_i[j, 0], lru_w_r[j, 1], lru_w_i[j, 1]],
                                 axis=-1).astype(BF16)
            b4 = jnp.stack([lru_b_r[j, 0], lru_b_i[j, 0], lru_b_r[j, 1], lru_b_i[j, 1]], axis=0)
            b4 = b4.reshape(4, HEADS, DH).transpose(1, 0, 2).reshape(HEADS, 1, 4 * DH)
            yc, yl = _lru_mixer(xr_c, gate_c, xr_l, gate_l, lru_conv_w[j],
                                lru_conv_b[j].reshape(1, d), w4, b4, lru_lambda[j])
            w_out = lru_w_out[j]
        else:
            w_in = hg_w_in[j].astype(BF16)
            parts_c = _hg_proj(xc, nw, mod_c, w_in, hg_lb_logits, i)
            parts_l = _hg_proj(xl, nw, mod_l, w_in, hg_lb_logits, i)
            yc, yl = _hg_mixer(parts_c, parts_l, hg_norm[j].reshape(1, DH))
            w_out = hg_w_out[j]

        nwf = norm_ffn[i].reshape(1, d)
        wo, w1, w3, w2 = (w.astype(BF16) for w in (w_out, ffn_w1[i], ffn_w3[i], ffn_w2[i]))
        xl = _post(xl, yl, mod_l, nwf, wo, w1, w3, w2)
        if i != depth - 1:
            xc = _post(xc, yc, mod_c, nwf, wo, w1, w3, w2)
        if col:
            xl = _from_scan_order(xl)
    return _final_norm(xl, norm_final.reshape(1, d))
```

```python
import functools
import math

import jax
import jax.numpy as jnp
from jax import lax
from jax.experimental import pallas as pl
from jax.experimental.pallas import tpu as pltpu

F32 = jnp.float32
BF16 = jnp.bfloat16

D_MODEL = 1024
HEADS = 8
DH = 128
CHUNK = 64
GRID_W = 64
EPS = 1e-6
LRU_C = 8.0
FFN_HIDDEN = 2816
FFN_COLS = 256
TOKEN_TILE = 512
V7X_VMEM_BYTES = 64 * 1024 * 1024
VMEM_LIMIT = V7X_VMEM_BYTES - 6 * 1024 * 1024


def _cparams(*sem):
    return pltpu.CompilerParams(dimension_semantics=sem, vmem_limit_bytes=VMEM_LIMIT)


def _mm(a, b):
    return jnp.dot(a, b, preferred_element_type=F32)


def _mm_nt(a, b):
    return lax.dot_general(a, b, (((1,), (1,)), ((), ())), preferred_element_type=F32)


def _mm_tn(a, b):
    return lax.dot_general(a, b, (((0,), (0,)), ((), ())), preferred_element_type=F32)


def _silu(x):
    return x * jax.nn.sigmoid(x)


def _softplus(x):
    return jnp.maximum(x, 0.0) + jnp.log(1.0 + jnp.exp(-jnp.abs(x)))


def _gelu_tanh(x):
    return 0.5 * x * (1.0 + jnp.tanh(math.sqrt(2.0 / math.pi) * (x + 0.044715 * (x * x * x))))


def _norm_mod(x, nw, shift, scale):
    ms = jnp.mean(x * x, axis=-1, keepdims=True)
    y = x * lax.rsqrt(ms + EPS) * nw
    return y * (1.0 + scale) + shift


def _shift_rows(x, k):
    n = x.shape[0]
    return x if k % n == 0 else pltpu.roll(x, (-k) % n, 0)


def _chunk_cumsum(x, rows_in_chunk, rev):
    k = 1
    while k < CHUNK:
        if rev:
            x = x + jnp.where(rows_in_chunk < CHUNK - k, _shift_rows(x, k), 0.0)
        else:
            x = x + jnp.where(rows_in_chunk >= k, _shift_rows(x, -k), 0.0)
        k *= 2
    return x


def _conv_chunk(ref, t, n_chunks, cw):
    first = isinstance(t, int) and t == 0
    last = isinstance(t, int) and t == n_chunks - 1
    halo = 16
    if first:
        win = ref[0, 0:CHUNK + halo, :].astype(F32)
        off = 0
    elif last:
        start = t * CHUNK - halo
        win = ref[0, start:start + CHUNK + halo, :].astype(F32)
        off = halo
    else:
        start = t * CHUNK - halo
        if not isinstance(t, int):
            start = pl.multiple_of(start, halo)
        win = ref[0, pl.ds(start, CHUNK + 2 * halo), :].astype(F32)
        off = halo
    rows = lax.broadcasted_iota(jnp.int32, (CHUNK, win.shape[1]), 0)

    def tap(k):
        return _shift_rows(win, k)[off:off + CHUNK]

    xm2, xm1, x0, xp1 = tap(-2), tap(-1), tap(0), tap(1)
    if first:
        xm2 = jnp.where(rows >= 2, xm2, 0.0)
        xm1 = jnp.where(rows >= 1, xm1, 0.0)
    if last:
        xp1 = jnp.where(rows < CHUNK - 1, xp1, 0.0)
    return cw[0:1] * xm2 + cw[1:2] * xm1 + cw[2:3] * x0 + cw[3:4] * xp1


def _for_chunks(n_chunks, fn):
    fn(0)
    if n_chunks > 2:
        def body(t, carry):
            fn(t)
            return carry
        lax.fori_loop(1, n_chunks - 1, body, 0)
    if n_chunks > 1:
        fn(n_chunks - 1)


def _rows(t):
    return pl.ds(pl.multiple_of(t * CHUNK, CHUNK), CHUNK)


def _gated_norm_out(of, ob, z_ref, y_ref, nw, base, n_chunks):
    def body(t, carry):
        r = _rows(t)
        rb = pl.ds(pl.multiple_of(base + t * CHUNK, CHUNK), CHUNK)
        o = of[rb, :] + ob[rb, :]
        ms = jnp.mean(o * o, axis=-1, keepdims=True)
        y = o * lax.rsqrt(ms + EPS) * nw
        y_ref[0, r, :] = (y * _silu(z_ref[0, r, :].astype(F32))).astype(y_ref.dtype)
        return carry
    lax.fori_loop(0, n_chunks, body, 0)


def _mods_kernel(cv_ref, w_ref, b_ref, o_ref):
    s = _silu(cv_ref[...])
    o_ref[0] = jnp.dot(s, w_ref[0], preferred_element_type=F32,
                       precision=lax.Precision.HIGHEST) + b_ref[0]


def _mods(cvec, ada_w, ada_b):
    depth, d, n = ada_w.shape
    tn = 1536
    return pl.pallas_call(
        _mods_kernel,
        grid=(depth, n // tn),
        in_specs=[
            pl.BlockSpec((8, d), lambda i, j: (0, 0)),
            pl.BlockSpec((1, d, tn), lambda i, j: (i, 0, j)),
            pl.BlockSpec((1, 1, tn), lambda i, j: (i, 0, j)),
        ],
        out_specs=pl.BlockSpec((1, 8, tn), lambda i, j: (i, 0, j)),
        out_shape=jax.ShapeDtypeStruct((depth, 8, n), F32),
        compiler_params=_cparams("parallel", "parallel"),
        name="adaln_mods",
    )(cvec, ada_w, ada_b.reshape(depth, 1, n))


def _token_tile(L):
    return min(TOKEN_TILE, L)


def _proj_in_specs(tm, d):
    return [
        pl.BlockSpec((1, tm, d), lambda b, t: (b, t, 0)),
        pl.BlockSpec((1, d), lambda b, t: (0, 0)),
        pl.BlockSpec((1, 6, d), lambda b, t: (b, 0, 0)),
    ]


def _full(shape):
    nd = len(shape)
    return pl.BlockSpec(shape, lambda b, t: (0,) * nd)


def _gdn_proj_kernel(x_ref, nw_ref, mod_ref, w_ref, wabh_ref, wabl_ref, alog_ref, dtb_ref,
                     qkvz_ref, grow_ref):
    h = _norm_mod(x_ref[0], nw_ref[...], mod_ref[0, 0:1], mod_ref[0, 1:2])
    hb = h.astype(BF16)
    n = w_ref.shape[1]
    for c in range(n // D_MODEL):
        cs = slice(c * D_MODEL, (c + 1) * D_MODEL)
        qkvz_ref[0, :, cs] = _mm(hb, w_ref[:, cs]).astype(qkvz_ref.dtype)
    hl = (h - hb.astype(F32)).astype(BF16)
    ab = _mm(hb, wabh_ref[...]) + _mm(hl, wabh_ref[...]) + _mm(hb, wabl_ref[...])
    lane = lax.broadcasted_iota(jnp.int32, ab.shape, 1)
    r64 = lax.broadcasted_iota(jnp.int32, ab.shape, 0) & (CHUNK - 1)
    g = -jnp.exp(alog_ref[...]) * _softplus(ab + dtb_ref[...])
    res = jnp.where(lane < HEADS, _chunk_cumsum(g, r64, False),
                    jnp.where(lane < 2 * HEADS, _chunk_cumsum(g, r64, True),
                              jax.nn.sigmoid(ab)))
    grow_ref[0] = res.T[0:4 * HEADS, :]


def _gdn_proj(x, nw, mods, w_main, wab_hi, wab_lo, alog, dtb):
    B, L, d = x.shape
    tm = _token_tile(L)
    n = w_main.shape[1]
    return pl.pallas_call(
        _gdn_proj_kernel,
        grid=(B, L // tm),
        in_specs=_proj_in_specs(tm, d) + [
            _full(w_main.shape), _full(wab_hi.shape), _full(wab_lo.shape),
            _full(alog.shape), _full(dtb.shape)],
        out_specs=[
            pl.BlockSpec((1, tm, n), lambda b, t: (b, t, 0)),
            pl.BlockSpec((1, 4 * HEADS, tm), lambda b, t: (b, 0, t)),
        ],
        out_shape=[
            jax.ShapeDtypeStruct((B, L, n), BF16),
            jax.ShapeDtypeStruct((B, 4 * HEADS, L), F32),
        ],
        compiler_params=_cparams("parallel", "parallel"),
        name="gdn_proj",
    )(x, nw, mods, w_main, wab_hi, wab_lo, alog, dtb)


def _lru_proj_kernel(x_ref, nw_ref, mod_ref, w_ref, gate_ref, xr_ref):
    h = _norm_mod(x_ref[0], nw_ref[...], mod_ref[0, 0:1], mod_ref[0, 1:2])
    hb = h.astype(BF16)
    gate_ref[0] = _gelu_tanh(_mm(hb, w_ref[:, 0:D_MODEL])).astype(gate_ref.dtype)
    xr_ref[0] = _mm(hb, w_ref[:, D_MODEL:2 * D_MODEL])


def _lru_proj(x, nw, mods, w):
    B, L, d = x.shape
    tm = _token_tile(L)
    spec = pl.BlockSpec((1, tm, d), lambda b, t: (b, t, 0))
    return pl.pallas_call(
        _lru_proj_kernel,
        grid=(B, L // tm),
        in_specs=_proj_in_specs(tm, d) + [_full(w.shape)],
        out_specs=[spec, spec],
        out_shape=[jax.ShapeDtypeStruct((B, L, d), BF16), jax.ShapeDtypeStruct((B, L, d), F32)],
        compiler_params=_cparams("parallel", "parallel"),
        name="lru_proj",
    )(x, nw, mods, w)


def _hg_proj_kernel(x_ref, nw_ref, mod_ref, w_ref, lbl_ref, q_ref, f0_ref, f1_ref, v_ref, z_ref,
                    *, layer):
    logits = lbl_ref[...]
    depth = logits.shape[0]
    m = logits[0:1]
    for r in range(1, depth):
        m = jnp.maximum(m, logits[r:r + 1])
    e = jnp.exp(logits - m)
    tot = e[0:1]
    for r in range(1, depth):
        tot = tot + e[r:r + 1]
    lb = jnp.zeros_like(tot)
    for r in range(1, layer + 1):
        lb = lb + e[r:r + 1] / tot

    h = _norm_mod(x_ref[0], nw_ref[...], mod_ref[0, 0:1], mod_ref[0, 1:2])
    hb = h.astype(BF16)

    def cols(c):
        return _mm(hb, w_ref[:, c * D_MODEL:(c + 1) * D_MODEL])

    q_ref[0] = _silu(cols(0)).astype(q_ref.dtype)
    for d, f_ref in enumerate((f0_ref, f1_ref)):
        f = lb + (1.0 - lb) * jax.nn.sigmoid(cols(1 + d))
        f_ref[0] = jnp.log(f)
    v_ref[0] = cols(3).astype(v_ref.dtype)
    z_ref[0] = cols(4).astype(z_ref.dtype)


def _hg_proj(x, nw, mods, w, lb_logits, layer):
    B, L, d = x.shape
    tm = _token_tile(L)
    spec = pl.BlockSpec((1, tm, d), lambda b, t: (b, t, 0))
    sds = lambda dt: jax.ShapeDtypeStruct((B, L, d), dt)
    return pl.pallas_call(
        functools.partial(_hg_proj_kernel, layer=layer),
        grid=(B, L // tm),
        in_specs=_proj_in_specs(tm, d) + [_full(w.shape), _full(lb_logits.shape)],
        out_specs=[spec] * 5,
        out_shape=[sds(BF16), sds(F32), sds(F32), sds(BF16), sds(BF16)],
        compiler_params=_cparams("parallel", "parallel"),
        name="hg_proj",
    )(x, nw, mods, w, lb_logits)


def _post_kernel(x_ref, y_ref, mod_ref, nw_ref, wo_ref, w1_ref, w3_ref, w2_ref, o_ref, acc_ref):
    x1 = x_ref[0] + mod_ref[0, 2:3] * _mm(y_ref[0], wo_ref[...])
    hb = _norm_mod(x1, nw_ref[...], mod_ref[0, 3:4], mod_ref[0, 4:5]).astype(BF16)
    hidden = w1_ref.shape[1]
    for j in range(hidden // FFN_COLS):
        cs = slice(j * FFN_COLS, (j + 1) * FFN_COLS)
        a = (_silu(_mm(hb, w1_ref[:, cs])) * _mm(hb, w3_ref[:, cs])).astype(BF16)
        part = _mm(a, w2_ref[cs, :])
        if j == 0:
            acc_ref[...] = part
        else:
            acc_ref[...] += part
    o_ref[0] = x1 + mod_ref[0, 5:6] * acc_ref[...]


def _post(x, y, mods, nw, wo, w1, w3, w2):
    B, L, d = x.shape
    tm = _token_tile(L)
    tile = pl.BlockSpec((1, tm, d), lambda b, t: (b, t, 0))

    def resident(shape):
        return pl.BlockSpec(shape, lambda b, t: (0, 0), pipeline_mode=pl.Buffered(1))

    return pl.pallas_call(
        _post_kernel,
        grid=(B, L // tm),
        in_specs=[tile, tile,
                  pl.BlockSpec((1, 6, d), lambda b, t: (b, 0, 0)),
                  pl.BlockSpec((1, d), lambda b, t: (0, 0)),
                  resident(wo.shape), resident(w1.shape), resident(w3.shape), resident(w2.shape)],
        out_specs=tile,
        out_shape=jax.ShapeDtypeStruct((B, L, d), F32),
        scratch_shapes=[pltpu.VMEM((tm, d), F32)],
        compiler_params=_cparams("parallel", "parallel"),
        name="post_ffn",
    )(x, y, mods, nw, wo, w1, w3, w2)


def _final_norm_kernel(x_ref, nw_ref, o_ref):
    x = x_ref[0]
    ms = jnp.mean(x * x, axis=-1, keepdims=True)
    o_ref[0] = x * lax.rsqrt(ms + EPS) * nw_ref[...]


def _final_norm(x, nw):
    B, L, d = x.shape
    tm = _token_tile(L)
    tile = pl.BlockSpec((1, tm, d), lambda b, t: (b, t, 0))
    return pl.pallas_call(
        _final_norm_kernel,
        grid=(B, L // tm),
        in_specs=[tile, pl.BlockSpec((1, d), lambda b, t: (0, 0))],
        out_specs=tile,
        out_shape=jax.ShapeDtypeStruct((B, L, d), F32),
        compiler_params=_cparams("parallel", "parallel"),
        name="final_norm",
    )(x, nw)


GDN_GROUP = 4


def _block_diag(m):
    left = lax.broadcasted_iota(jnp.int32, m.shape, 1) < CHUNK
    return jnp.concatenate([jnp.where(left, m, 0.0), jnp.where(left, 0.0, m)], axis=0).astype(BF16)


def _dup_diag(m16):
    z = jnp.zeros_like(m16)
    return jnp.concatenate([jnp.concatenate([m16, z], axis=1),
                            jnp.concatenate([z, m16], axis=1)], axis=0)


def _packed_masks():
    ii = lax.broadcasted_iota(jnp.int32, (CHUNK, 2 * CHUNK), 0)
    jj = lax.broadcasted_iota(jnp.int32, (CHUNK, 2 * CHUNK), 1)
    jl = jj & (CHUNK - 1)
    bwd = jj >= CHUNK
    fwd = jnp.logical_not(bwd)
    incl = (fwd & (ii >= jl)) | (bwd & (ii <= jl))
    strict = (fwd & (ii > jl)) | (bwd & (ii < jl))
    same = {b: (ii & -b) == (jl & -b) for b in (4, 8, 16, 32, 64)}
    return dict(left=fwd, incl=incl, strict=strict, eye=ii == jl, same=same)


def _tri_inverse_packed(nms, masks):
    same = masks["same"]
    n4 = [jnp.where(same[4], n, 0.0) for n in nms]
    n4d = [_block_diag(a) for a in n4]
    p2 = [_mm(a.astype(BF16), d) for a, d in zip(n4, n4d)]
    p3 = [_mm(p.astype(BF16), d) for p, d in zip(p2, n4d)]
    eye = jnp.where(masks["eye"], 1.0, 0.0)
    xs = [eye + a + b + c for a, b, c in zip(n4, p2, p3)]
    b = 4
    while b < CHUNK:
        off_block = same[2 * b] & jnp.logical_not(same[b])
        cms = [jnp.where(off_block, n, 0.0).astype(BF16) for n in nms]
        xds = [_block_diag(x) for x in xs]
        yds = [_block_diag(_mm(c, d)) for c, d in zip(cms, xds)]
        xs = [x + _mm(x.astype(BF16), d) for x, d in zip(xs, yds)]
        b *= 2
    return xs


def _gdn_precompute(q_ref, k_ref, v_ref, gp_ref, cwq, cwk, cwv, base, n_chunks,
                    w_s, qe_s, ke_s, u_s, a_s, dl_s):
    masks = _packed_masks()
    left, incl, strict = masks["left"], masks["incl"], masks["strict"]

    def group(t0):
        ts = [t0 + g for g in range(GDN_GROUP)]
        tiles = [gp_ref[0, 0, t] for t in ts]
        cols = [tl.T for tl in tiles]
        qs, ks, v16s = [], [], []
        for t in ts:
            q = _silu(_conv_chunk(q_ref, t, n_chunks, cwq))
            qs.append(q * lax.rsqrt(jnp.sum(q * q, axis=-1, keepdims=True) + EPS) * (DH ** -0.5))
            k = _silu(_conv_chunk(k_ref, t, n_chunks, cwk))
            ks.append(k * lax.rsqrt(jnp.sum(k * k, axis=-1, keepdims=True) + EPS))
            v16s.append(_silu(_conv_chunk(v_ref, t, n_chunks, cwv)).astype(BF16))
        k16s = [k.astype(BF16) for k in ks]
        q16s = [q.astype(BF16) for q in qs]
        kqs = [_mm_nt(jnp.concatenate([k, q], axis=0), jnp.concatenate([k, k], axis=0))
               for k, q in zip(k16s, q16s)]
        g_rows = [tl[0:1, :] for tl in tiles]
        b_rows = [tl[1:2, :] for tl in tiles]
        g_cols = [jnp.where(left, c[0:CHUNK, 0:1], c[CHUNK:2 * CHUNK, 0:1]) for c in cols]
        b_cols = [jnp.where(left, c[0:CHUNK, 1:2], c[CHUNK:2 * CHUNK, 1:2]) for c in cols]
        decays = [jnp.where(incl, jnp.exp(jnp.where(incl, gc - gr, 0.0)), 0.0)
                  for gc, gr in zip(g_cols, g_rows)]
        nms = [jnp.where(strict, -(kq[0:CHUNK] * bc * dc), 0.0)
               for kq, bc, dc in zip(kqs, b_cols, decays)]
        tinvs = _tri_inverse_packed(nms, masks)
        tus = [(x * br).astype(BF16) for x, br in zip(tinvs, b_rows)]
        tws = [(x * (br * jnp.exp(gr))).astype(BF16) for x, br, gr in zip(tinvs, b_rows, g_rows)]
        uus = [_mm(tu, _dup_diag(v)) for tu, v in zip(tus, v16s)]
        wws = [_mm(tw, _dup_diag(k)) for tw, k in zip(tws, k16s)]
        for g, t in enumerate(ts):
            row0 = base + t * CHUNK
            dst = pl.ds(row0, CHUNK) if isinstance(t, int) else \
                pl.ds(pl.multiple_of(row0, CHUNK), CHUNK)
            a_s[dst, :] = jnp.where(incl, kqs[g][CHUNK:2 * CHUNK] * decays[g], 0.0).astype(BF16)
            ci = base // CHUNK + t
            for d in range(2):
                g_col = cols[g][d * CHUNK:(d + 1) * CHUNK, 0:1]
                g_tot = tiles[g][0:1, CHUNK:CHUNK + 1] if d else tiles[g][0:1, CHUNK - 1:CHUNK]
                u_s[d, dst, :] = uus[g][:, d * DH:(d + 1) * DH]
                w_s[d, dst, :] = wws[g][:, d * DH:(d + 1) * DH].astype(BF16)
                qe_s[d, dst, :] = (qs[g] * jnp.exp(g_col)).astype(BF16)
                ke_s[d, dst, :] = (ks[g] * jnp.exp(g_tot - g_col)).astype(BF16)
                dl_s[d, pl.ds(ci, 1), :] = jnp.broadcast_to(jnp.exp(g_tot), (1, DH))

    n_groups = n_chunks // GDN_GROUP
    group(0)
    if n_groups > 2:
        def body(i, carry):
            group(i * GDN_GROUP)
            return carry
        lax.fori_loop(1, n_groups - 1, body, 0)
    if n_groups > 1:
        group((n_groups - 1) * GDN_GROUP)


def _gdn_mixer_kernel(qc, kc, vc, zc, gpc, ql, kl, vl, zl, gpl, cwq, cwk, cwv, nw, yc, yl,
                      w_s, qe_s, ke_s, u_s, a_s, dl_s, oh, sf, sb, *, Lc, Ll):
    nc, nl = Lc // CHUNK, Ll // CHUNK
    cwq_v, cwk_v, cwv_v = cwq[...], cwk[...], cwv[...]
    scr = (w_s, qe_s, ke_s, u_s, a_s, dl_s)
    _gdn_precompute(qc, kc, vc, gpc, cwq_v, cwk_v, cwv_v, 0, nc, *scr)
    _gdn_precompute(ql, kl, vl, gpl, cwq_v, cwk_v, cwv_v, Lc, nl, *scr)
    sf[...] = jnp.zeros_like(sf)
    sb[...] = jnp.zeros_like(sb)
    nw_v = nw[...]
    s_refs = (sf, sb)
    zero_half = jnp.zeros((CHUNK, DH), BF16)

    def scan(z_ref, y_ref, base, n):
        def steps(cs):
            rbs = [pl.ds(pl.multiple_of(base + c * CHUNK, CHUNK), CHUNK) for c in cs]
            ss = [r[...] for r in s_refs]
            wqs = [_mm(jnp.concatenate([w_s[d, rbs[d], :], qe_s[d, rbs[d], :]], axis=0),
                       ss[d].astype(BF16)) for d in range(2)]
            vns = [(u_s[d, rbs[d], :] - wqs[d][0:CHUNK]).astype(BF16) for d in range(2)]
            vpad = [jnp.concatenate([vns[0], zero_half], axis=0),
                    jnp.concatenate([zero_half, vns[1]], axis=0)]
            outs = [wqs[d][CHUNK:2 * CHUNK] + _mm(a_s[rbs[d], :], vpad[d]) for d in range(2)]
            for d in range(2):
                dl = dl_s[d, pl.ds(base // CHUNK + cs[d], 1), :]
                s_refs[d][...] = ss[d] * dl + _mm_tn(ke_s[d, rbs[d], :], vns[d])
            return rbs, outs

        def first_visit(cs):
            rbs, outs = steps(cs)
            for d in range(2):
                oh[rbs[d], :] = outs[d]

        def second_visit(cs):
            rbs, outs = steps(cs)
            for d in range(2):
                o = outs[d] + oh[rbs[d], :]
                y = o * lax.rsqrt(jnp.mean(o * o, axis=-1, keepdims=True) + EPS) * nw_v
                r = _rows(cs[d])
                y_ref[0, r, :] = (y * _silu(z_ref[0, r, :].astype(F32))).astype(y_ref.dtype)

        def loop(lo, hi, visit):
            def body(i, carry):
                visit((i, n - 1 - i))
                return carry
            lax.fori_loop(lo, hi, body, 0)

        loop(0, n // 2, first_visit)
        loop(n // 2, n, second_visit)

    scan(zc, yc, 0, nc)
    scan(zl, yl, Lc, nl)


def _gdn_mixer(pc, grc, plat, grl, conv_w, norm_w):
    B, Lc, _ = pc.shape
    Ll = plat.shape[1]

    def pack_rows(gr, L):
        g = gr.reshape(B, 2, 2, HEADS, L // CHUNK, CHUNK).transpose(0, 3, 4, 1, 2, 5)
        g = g.reshape(B, HEADS, L // CHUNK, 2, 2 * CHUNK)
        return jnp.pad(g, ((0, 0), (0, 0), (0, 0), (0, 6), (0, 0)))

    def seq_specs(L, mode):
        head_blk = lambda k: pl.BlockSpec((1, L, DH), lambda b, h, k=k: (b, 0, k * HEADS + h),
                                          pipeline_mode=mode)
        return [head_blk(0), head_blk(1), head_blk(2), head_blk(3),
                pl.BlockSpec((1, 1, L // CHUNK, 8, 2 * CHUNK), lambda b, h: (b, h, 0, 0, 0))]

    cw_blk = lambda k: pl.BlockSpec((4, DH), lambda b, h, k=k: (0, k * HEADS + h))
    y_spec = lambda L: pl.BlockSpec((1, L, DH), lambda b, h: (b, 0, h))
    Lt = Lc + Ll
    per_dir = lambda w, dt: pltpu.VMEM((2, Lt, w), dt)
    return pl.pallas_call(
        functools.partial(_gdn_mixer_kernel, Lc=Lc, Ll=Ll),
        grid=(B, HEADS),
        in_specs=seq_specs(Lc, None) + seq_specs(Ll, pl.Buffered(1)) + [
            cw_blk(0), cw_blk(1), cw_blk(2), pl.BlockSpec((1, DH), lambda b, h: (0, 0))],
        out_specs=[y_spec(Lc), y_spec(Ll)],
        out_shape=[jax.ShapeDtypeStruct((B, Lc, HEADS * DH), BF16),
                   jax.ShapeDtypeStruct((B, Ll, HEADS * DH), BF16)],
        scratch_shapes=[per_dir(DH, BF16), per_dir(DH, BF16), per_dir(DH, BF16), per_dir(DH, F32),
                        pltpu.VMEM((Lt, 2 * CHUNK), BF16), pltpu.VMEM((2, Lt // CHUNK, DH), F32),
                        pltpu.VMEM((Lt, DH), F32), pltpu.VMEM((DH, DH), F32),
                        pltpu.VMEM((DH, DH), F32)],
        compiler_params=_cparams("parallel", "arbitrary"),
        name="gdn_mixer",
    )(pc, pc, pc, pc, pack_rows(grc, Lc), plat, plat, plat, plat, pack_rows(grl, Ll),
      conv_w, conv_w, conv_w, norm_w)


def _hg_chunk(q16, logf, v16, st_ref, rev):
    rows = lax.broadcasted_iota(jnp.int32, (CHUNK, DH), 0)
    gcum = _chunk_cumsum(logf, rows, rev)
    mid = CHUNK // 2 - 1 if rev else CHUNK // 2
    end = 0 if rev else CHUNK - 1
    ref = gcum[mid:mid + 1, :]
    glast = gcum[end:end + 1, :]
    q = q16.astype(F32)
    k = 1.0 - jnp.exp(logf)
    ii = lax.broadcasted_iota(jnp.int32, (CHUNK, CHUNK), 0)
    jj = lax.broadcasted_iota(jnp.int32, (CHUNK, CHUNK), 1)
    incl = (ii <= jj) if rev else (ii >= jj)
    scores = _mm_nt((q * jnp.exp(gcum - ref)).astype(BF16), (k * jnp.exp(ref - gcum)).astype(BF16))
    scores = jnp.where(incl, scores, 0.0).astype(BF16)
    st = st_ref[...]
    out = _mm(scores, v16) + _mm_nt((q * jnp.exp(gcum)).astype(BF16), st.astype(BF16))
    kd = (k * jnp.exp(glast - gcum)).astype(BF16)
    st_ref[...] = st * jnp.exp(glast) + _mm_tn(v16, kd)
    return out


def _hg_mixer_kernel(qc, f0c, f1c, vc, zc, ql, f0l, f1l, vl, zl, nw, yc, yl, of, ob, sf, sb,
                     *, Lc, Ll):
    nc, nl = Lc // CHUNK, Ll // CHUNK
    sf[...] = jnp.zeros_like(sf)
    sb[...] = jnp.zeros_like(sb)

    def scan(q_ref, f0_ref, f1_ref, v_ref, base, n):
        def one(t, f_ref, s_ref, o_ref, rev):
            r = _rows(t)
            rb = pl.ds(pl.multiple_of(base + t * CHUNK, CHUNK), CHUNK)
            o_ref[rb, :] = _hg_chunk(q_ref[0, r, :], f_ref[0, r, :], v_ref[0, r, :], s_ref, rev)

        def body(i, carry):
            one(i, f0_ref, sf, of, False)
            one(n - 1 - i, f1_ref, sb, ob, True)
            return carry
        lax.fori_loop(0, n, body, 0)

    scan(qc, f0c, f1c, vc, 0, nc)
    scan(ql, f0l, f1l, vl, Lc, nl)
    nw_v = nw[...]
    _gated_norm_out(of, ob, zc, yc, nw_v, 0, nc)
    _gated_norm_out(of, ob, zl, yl, nw_v, Lc, nl)


def _hg_mixer(ctx_parts, lat_parts, norm_w):
    B, Lc, _ = ctx_parts[0].shape
    Ll = lat_parts[0].shape[1]
    blk = lambda L: pl.BlockSpec((1, L, DH), lambda b, h: (b, 0, h))
    Lt = Lc + Ll
    return pl.pallas_call(
        functools.partial(_hg_mixer_kernel, Lc=Lc, Ll=Ll),
        grid=(B, HEADS),
        in_specs=[blk(Lc)] * 5 + [blk(Ll)] * 5 + [pl.BlockSpec((1, DH), lambda b, h: (0, 0))],
        out_specs=[blk(Lc), blk(Ll)],
        out_shape=[jax.ShapeDtypeStruct((B, Lc, HEADS * DH), BF16),
                   jax.ShapeDtypeStruct((B, Ll, HEADS * DH), BF16)],
        scratch_shapes=[pltpu.VMEM((Lt, DH), F32)] * 2 + [pltpu.VMEM((DH, DH), F32)] * 2,
        compiler_params=_cparams("parallel", "arbitrary"),
        name="hg_mixer",
    )(*ctx_parts, *lat_parts, norm_w)


def _lru_chunk(x, w_ref, b_ref, sp, h0, d):
    rev = d == 1
    cs = slice(2 * DH * d, 2 * DH * (d + 1))
    gates = _mm(x.astype(BF16), w_ref[0, :, cs]) + b_ref[0, :, cs]
    r = jax.nn.sigmoid(gates[:, 0:DH])
    ig = jax.nn.sigmoid(gates[:, DH:2 * DH])
    log_a = -LRU_C * r * sp
    a = jnp.exp(log_a)
    bv = jnp.sqrt(1.0 - jnp.exp(2.0 * log_a)) * (ig * x)
    rows = lax.broadcasted_iota(jnp.int32, (CHUNK, DH), 0)
    k = 1
    while k < CHUNK:
        if rev:
            m = rows < CHUNK - k
            a_sh, b_sh = _shift_rows(a, k), _shift_rows(bv, k)
        else:
            m = rows >= k
            a_sh, b_sh = _shift_rows(a, -k), _shift_rows(bv, -k)
        bv = jnp.where(m, bv + a * b_sh, bv)
        a = jnp.where(m, a * a_sh, a)
        k *= 2
    hrows = bv + a * h0
    end = 0 if rev else CHUNK - 1
    return hrows, hrows[end:end + 1, :]


def _lru_mixer_kernel(xc, gc, xl, gl, cw, cb, w4, b4, lam, yc, yl, xs, of, ob, *, Lc, Ll):
    nc, nl = Lc // CHUNK, Ll // CHUNK
    cw_v = cw[...]
    cb_v = cb[...]

    def prep(x_ref, base, n):
        def fn(t):
            dst = pl.ds(base + t * CHUNK, CHUNK) if isinstance(t, int) else \
                pl.ds(pl.multiple_of(base + t * CHUNK, CHUNK), CHUNK)
            xs[dst, :] = _conv_chunk(x_ref, t, n, cw_v) + cb_v
        _for_chunks(n, fn)

    prep(xc, 0, nc)
    prep(xl, Lc, nl)
    sp0 = _softplus(-lam[0:1, :])
    sp1 = _softplus(-lam[1:2, :])

    def scan(base, n, carry):
        def body(i, c):
            hf, hb = c
            rf = pl.ds(pl.multiple_of(base + i * CHUNK, CHUNK), CHUNK)
            rbk = pl.ds(pl.multiple_of(base + (n - 1 - i) * CHUNK, CHUNK), CHUNK)
            yf, hf = _lru_chunk(xs[rf, :], w4, b4, sp0, hf, 0)
            of[rf, :] = yf
            yb, hb = _lru_chunk(xs[rbk, :], w4, b4, sp1, hb, 1)
            ob[rbk, :] = yb
            return hf, hb
        return lax.fori_loop(0, n, body, carry)

    zero = jnp.zeros((1, DH), F32)
    carry = scan(0, nc, (zero, zero))
    scan(Lc, nl, carry)

    def out(g_ref, y_ref, base, n):
        def body(t, c):
            r = _rows(t)
            rb = pl.ds(pl.multiple_of(base + t * CHUNK, CHUNK), CHUNK)
            y_ref[0, r, :] = (g_ref[0, r, :].astype(F32) * (of[rb, :] + ob[rb, :])).astype(y_ref.dtype)
            return c
        lax.fori_loop(0, n, body, 0)

    out(gc, yc, 0, nc)
    out(gl, yl, Lc, nl)


def _lru_mixer(xr_c, gate_c, xr_l, gate_l, conv_w, conv_b, w4, b4, lam):
    B, Lc, _ = xr_c.shape
    Ll = xr_l.shape[1]
    blk = lambda L: pl.BlockSpec((1, L, DH), lambda b, g: (b, 0, g))
    Lt = Lc + Ll
    return pl.pallas_call(
        functools.partial(_lru_mixer_kernel, Lc=Lc, Ll=Ll),
        grid=(B, HEADS),
        in_specs=[blk(Lc), blk(Lc), blk(Ll), blk(Ll),
                  pl.BlockSpec((4, DH), lambda b, g: (0, g)),
                  pl.BlockSpec((1, DH), lambda b, g: (0, g)),
                  pl.BlockSpec((1, DH, 4 * DH), lambda b, g: (g, 0, 0)),
                  pl.BlockSpec((1, 1, 4 * DH), lambda b, g: (g, 0, 0)),
                  pl.BlockSpec((2, DH), lambda b, g: (0, g))],
        out_specs=[blk(Lc), blk(Ll)],
        out_shape=[jax.ShapeDtypeStruct((B, Lc, HEADS * DH), BF16),
                   jax.ShapeDtypeStruct((B, Ll, HEADS * DH), BF16)],
        scratch_shapes=[pltpu.VMEM((Lt, DH), F32)] * 3,
        compiler_params=_cparams("parallel", "arbitrary"),
        name="lru_mixer",
    )(xr_c, gate_c, xr_l, gate_l, conv_w, conv_b, w4, b4, lam)


def _to_scan_order(h):
    bsz, L, d = h.shape
    return h.reshape(bsz, L // GRID_W, GRID_W, d).transpose(0, 2, 1, 3).reshape(bsz, L, d)


def _from_scan_order(h):
    bsz, L, d = h.shape
    return h.reshape(bsz, GRID_W, L // GRID_W, d).transpose(0, 2, 1, 3).reshape(bsz, L, d)


def _pad_lanes(v, layout):
    row = jnp.zeros((DH,), F32)
    for off in layout:
        row = row.at[off:off + v.shape[0]].set(v)
    return row.reshape(1, DH)


def kernel(x, c, ctx, c_ctx, ada_w, ada_b, norm_mix, norm_ffn, norm_final, ffn_w1, ffn_w3, ffn_w2, gdn_w_in, gdn_conv, gdn_a_log, gdn_dt_bias, gdn_norm, gdn_w_out, lru_w_in, lru_conv_w, lru_conv_b, lru_w_r, lru_b_r, lru_w_i, lru_b_i, lru_lambda, lru_w_out, hg_w_in, hg_lb_logits, hg_norm, hg_w_out):
    B, L, d = x.shape
    depth = ada_w.shape[0]
    assert d == D_MODEL and L % GRID_W == 0 and L % CHUNK == 0 and ctx.shape[1] % CHUNK == 0
    assert B + 1 <= 8

    cvec = jnp.zeros((8, d), F32).at[0:B].set(c).at[B].set(c_ctx)
    mods = _mods(cvec, ada_w, ada_b).reshape(depth, 8, 6, d)

    xl, xc = x, ctx
    for i in range(depth):
        mod_l = mods[i, 0:B]
        mod_c = jnp.broadcast_to(mods[i, B][None], (B, 6, d))
        col = i % 2 == 1
        if col:
            xl = _to_scan_order(xl)
        nw = norm_mix[i].reshape(1, d)
        kind, j = i % 3, i // 3
        if kind == 0:
            w_in = gdn_w_in[j]
            n_main = 4 * HEADS * DH
            w_main = w_in[:, :n_main].astype(BF16)
            wa = w_in[:, n_main:n_main + 2 * HEADS]
            wb = w_in[:, n_main + 2 * HEADS:n_main + 4 * HEADS]
            wab = jnp.zeros((d, DH), F32)
            wab = wab.at[:, 0:2 * HEADS].set(wa).at[:, 2 * HEADS:4 * HEADS].set(wb)
            wab_hi = wab.astype(BF16)
            wab_lo = (wab - wab_hi.astype(F32)).astype(BF16)
            alog = _pad_lanes(gdn_a_log[j].reshape(-1), (0,))
            dtb = _pad_lanes(gdn_dt_bias[j].reshape(-1), (0,))
            pc, grc = _gdn_proj(xc, nw, mod_c, w_main, wab_hi, wab_lo, alog, dtb)
            plat, grl = _gdn_proj(xl, nw, mod_l, w_main, wab_hi, wab_lo, alog, dtb)
            yc, yl = _gdn_mixer(pc, grc, plat, grl, gdn_conv[j], gdn_norm[j].reshape(1, DH))
            w_out = gdn_w_out[j]
        elif kind == 1:
            w_in = lru_w_in[j].astype(BF16)
            gate_c, xr_c = _lru_proj(xc, nw, mod_c, w_in)
            gate_l, xr_l = _lru_proj(xl, nw, mod_l, w_in)
            w4 = jnp.concatenate([lru_w_r[j, 0], lru_w_i[j, 0], lru_w_r[j, 1], lru_w_i[j, 1]],
                                 axis=-1).astype(BF16)
            b4 = jnp.stack([lru_b_r[j, 0], lru_b_i[j, 0], lru_b_r[j, 1], lru_b_i[j, 1]], axis=0)
            b4 = b4.reshape(4, HEADS, DH).transpose(1, 0, 2).reshape(HEADS, 1, 4 * DH)
            yc, yl = _lru_mixer(xr_c, gate_c, xr_l, gate_l, lru_conv_w[j],
                                lru_conv_b[j].reshape(1, d), w4, b4, lru_lambda[j])
            w_out = lru_w_out[j]
        else:
            w_in = hg_w_in[j].astype(BF16)
            parts_c = _hg_proj(xc, nw, mod_c, w_in, hg_lb_logits, i)
            parts_l = _hg_proj(xl, nw, mod_l, w_in, hg_lb_logits, i)
            yc, yl = _hg_mixer(parts_c, parts_l, hg_norm[j].reshape(1, DH))
            w_out = hg_w_out[j]

        nwf = norm_ffn[i].reshape(1, d)
        wo, w1, w3, w2 = (w.astype(BF16) for w in (w_out, ffn_w1[i], ffn_w3[i], ffn_w2[i]))
        xl = _post(xl, yl, mod_l, nwf, wo, w1, w3, w2)
        if i != depth - 1:
            xc = _post(xc, yc, mod_c, nwf, wo, w1, w3, w2)
        if col:
            xl = _from_scan_order(xl)
    return _final_norm(xl, norm_final.reshape(1, d))
```

```python
import functools
import math

import jax
import jax.numpy as jnp
from jax import lax
from jax.experimental import pallas as pl
from jax.experimental.pallas import tpu as pltpu

F32 = jnp.float32
BF16 = jnp.bfloat16

D_MODEL = 1024
HEADS = 8
DH = 128
CHUNK = 64
GRID_W = 64
EPS = 1e-6
LRU_C = 8.0
FFN_HIDDEN = 2816
FFN_COLS = 256
TOKEN_TILE = 512
V7X_VMEM_BYTES = 64 * 1024 * 1024
VMEM_LIMIT = V7X_VMEM_BYTES - 6 * 1024 * 1024


def _cparams(*sem):
    return pltpu.CompilerParams(dimension_semantics=sem, vmem_limit_bytes=VMEM_LIMIT)


def _mm(a, b):
    return jnp.dot(a, b, preferred_element_type=F32)


def _mm_nt(a, b):
    return lax.dot_general(a, b, (((1,), (1,)), ((), ())), preferred_element_type=F32)


def _mm_tn(a, b):
    return lax.dot_general(a, b, (((0,), (0,)), ((), ())), preferred_element_type=F32)


def _silu(x):
    return x * jax.nn.sigmoid(x)


def _softplus(x):
    return jnp.maximum(x, 0.0) + jnp.log(1.0 + jnp.exp(-jnp.abs(x)))


def _gelu_tanh(x):
    return 0.5 * x * (1.0 + jnp.tanh(math.sqrt(2.0 / math.pi) * (x + 0.044715 * (x * x * x))))


def _norm_mod(x, nw, shift, scale):
    ms = jnp.mean(x * x, axis=-1, keepdims=True)
    y = x * lax.rsqrt(ms + EPS) * nw
    return y * (1.0 + scale) + shift


def _shift_rows(x, k):
    n = x.shape[0]
    return x if k % n == 0 else pltpu.roll(x, (-k) % n, 0)


def _chunk_cumsum(x, rows_in_chunk, rev):
    k = 1
    while k < CHUNK:
        if rev:
            x = x + jnp.where(rows_in_chunk < CHUNK - k, _shift_rows(x, k), 0.0)
        else:
            x = x + jnp.where(rows_in_chunk >= k, _shift_rows(x, -k), 0.0)
        k *= 2
    return x


def _conv_chunk(ref, t, n_chunks, cw):
    first = isinstance(t, int) and t == 0
    last = isinstance(t, int) and t == n_chunks - 1
    halo = 16
    if first:
        win = ref[0, 0:CHUNK + halo, :].astype(F32)
        off = 0
    elif last:
        start = t * CHUNK - halo
        win = ref[0, start:start + CHUNK + halo, :].astype(F32)
        off = halo
    else:
        start = t * CHUNK - halo
        if not isinstance(t, int):
            start = pl.multiple_of(start, halo)
        win = ref[0, pl.ds(start, CHUNK + 2 * halo), :].astype(F32)
        off = halo
    rows = lax.broadcasted_iota(jnp.int32, (CHUNK, win.shape[1]), 0)

    def tap(k):
        return _shift_rows(win, k)[off:off + CHUNK]

    xm2, xm1, x0, xp1 = tap(-2), tap(-1), tap(0), tap(1)
    if first:
        xm2 = jnp.where(rows >= 2, xm2, 0.0)
        xm1 = jnp.where(rows >= 1, xm1, 0.0)
    if last:
        xp1 = jnp.where(rows < CHUNK - 1, xp1, 0.0)
    return cw[0:1] * xm2 + cw[1:2] * xm1 + cw[2:3] * x0 + cw[3:4] * xp1


def _for_chunks(n_chunks, fn):
    fn(0)
    if n_chunks > 2:
        def body(t, carry):
            fn(t)
            return carry
        lax.fori_loop(1, n_chunks - 1, body, 0)
    if n_chunks > 1:
        fn(n_chunks - 1)


def _rows(t, base=0):
    start = base + t * CHUNK
    if not isinstance(start, int):
        start = pl.multiple_of(start, CHUNK)
    return pl.ds(start, CHUNK)


def _gated_norm_out(of, ob, z_ref, y_ref, nw, base, n_chunks):
    def body(t, carry):
        r = _rows(t)
        rb = pl.ds(pl.multiple_of(base + t * CHUNK, CHUNK), CHUNK)
        o = of[rb, :] + ob[rb, :]
        ms = jnp.mean(o * o, axis=-1, keepdims=True)
        y = o * lax.rsqrt(ms + EPS) * nw
        y_ref[0, r, :] = (y * _silu(z_ref[0, r, :].astype(F32))).astype(y_ref.dtype)
        return carry
    lax.fori_loop(0, n_chunks, body, 0)


def _mods_kernel(cv_ref, w_ref, b_ref, o_ref):
    s = _silu(cv_ref[...])
    o_ref[0] = jnp.dot(s, w_ref[0], preferred_element_type=F32,
                       precision=lax.Precision.HIGHEST) + b_ref[0]


def _mods(cvec, ada_w, ada_b):
    depth, d, n = ada_w.shape
    tn = 1536
    return pl.pallas_call(
        _mods_kernel,
        grid=(depth, n // tn),
        in_specs=[
            pl.BlockSpec((8, d), lambda i, j: (0, 0)),
            pl.BlockSpec((1, d, tn), lambda i, j: (i, 0, j)),
            pl.BlockSpec((1, 1, tn), lambda i, j: (i, 0, j)),
        ],
        out_specs=pl.BlockSpec((1, 8, tn), lambda i, j: (i, 0, j)),
        out_shape=jax.ShapeDtypeStruct((depth, 8, n), F32),
        compiler_params=_cparams("parallel", "parallel"),
        name="adaln_mods",
    )(cvec, ada_w, ada_b.reshape(depth, 1, n))


def _token_tile(L):
    return min(TOKEN_TILE, L)


def _proj_in_specs(tm, d):
    return [
        pl.BlockSpec((1, tm, d), lambda b, t: (b, t, 0)),
        pl.BlockSpec((1, d), lambda b, t: (0, 0)),
        pl.BlockSpec((1, 6, d), lambda b, t: (b, 0, 0)),
    ]


def _full(shape):
    nd = len(shape)
    return pl.BlockSpec(shape, lambda b, t: (0,) * nd)


def _gdn_proj_kernel(x_ref, nw_ref, mod_ref, w_ref, wabh_ref, wabl_ref, alog_ref, dtb_ref,
                     qkvz_ref, grow_ref):
    h = _norm_mod(x_ref[0], nw_ref[...], mod_ref[0, 0:1], mod_ref[0, 1:2])
    hb = h.astype(BF16)
    n = w_ref.shape[1]
    for c in range(n // D_MODEL):
        cs = slice(c * D_MODEL, (c + 1) * D_MODEL)
        qkvz_ref[0, :, cs] = _mm(hb, w_ref[:, cs]).astype(qkvz_ref.dtype)
    hl = (h - hb.astype(F32)).astype(BF16)
    ab = _mm(hb, wabh_ref[...]) + _mm(hl, wabh_ref[...]) + _mm(hb, wabl_ref[...])
    lane = lax.broadcasted_iota(jnp.int32, ab.shape, 1)
    r64 = lax.broadcasted_iota(jnp.int32, ab.shape, 0) & (CHUNK - 1)
    g = -jnp.exp(alog_ref[...]) * _softplus(ab + dtb_ref[...])
    res = jnp.where(lane < HEADS, _chunk_cumsum(g, r64, False),
                    jnp.where(lane < 2 * HEADS, _chunk_cumsum(g, r64, True),
                              jax.nn.sigmoid(ab)))
    grow_ref[0] = res.T[0:4 * HEADS, :]


def _gdn_proj(x, nw, mods, w_main, wab_hi, wab_lo, alog, dtb):
    B, L, d = x.shape
    tm = _token_tile(L)
    n = w_main.shape[1]
    return pl.pallas_call(
        _gdn_proj_kernel,
        grid=(B, L // tm),
        in_specs=_proj_in_specs(tm, d) + [
            _full(w_main.shape), _full(wab_hi.shape), _full(wab_lo.shape),
            _full(alog.shape), _full(dtb.shape)],
        out_specs=[
            pl.BlockSpec((1, tm, n), lambda b, t: (b, t, 0)),
            pl.BlockSpec((1, 4 * HEADS, tm), lambda b, t: (b, 0, t)),
        ],
        out_shape=[
            jax.ShapeDtypeStruct((B, L, n), BF16),
            jax.ShapeDtypeStruct((B, 4 * HEADS, L), F32),
        ],
        compiler_params=_cparams("parallel", "parallel"),
        name="gdn_proj",
    )(x, nw, mods, w_main, wab_hi, wab_lo, alog, dtb)


def _lru_proj_kernel(x_ref, nw_ref, mod_ref, w_ref, gate_ref, xr_ref):
    h = _norm_mod(x_ref[0], nw_ref[...], mod_ref[0, 0:1], mod_ref[0, 1:2])
    hb = h.astype(BF16)
    gate_ref[0] = _gelu_tanh(_mm(hb, w_ref[:, 0:D_MODEL])).astype(gate_ref.dtype)
    xr_ref[0] = _mm(hb, w_ref[:, D_MODEL:2 * D_MODEL])


def _lru_proj(x, nw, mods, w):
    B, L, d = x.shape
    tm = _token_tile(L)
    spec = pl.BlockSpec((1, tm, d), lambda b, t: (b, t, 0))
    return pl.pallas_call(
        _lru_proj_kernel,
        grid=(B, L // tm),
        in_specs=_proj_in_specs(tm, d) + [_full(w.shape)],
        out_specs=[spec, spec],
        out_shape=[jax.ShapeDtypeStruct((B, L, d), BF16), jax.ShapeDtypeStruct((B, L, d), F32)],
        compiler_params=_cparams("parallel", "parallel"),
        name="lru_proj",
    )(x, nw, mods, w)


def _hg_proj_kernel(x_ref, nw_ref, mod_ref, w_ref, lbl_ref, q_ref, f0_ref, f1_ref, v_ref, z_ref,
                    *, layer):
    logits = lbl_ref[...]
    depth = logits.shape[0]
    m = logits[0:1]
    for r in range(1, depth):
        m = jnp.maximum(m, logits[r:r + 1])
    e = jnp.exp(logits - m)
    tot = e[0:1]
    for r in range(1, depth):
        tot = tot + e[r:r + 1]
    lb = jnp.zeros_like(tot)
    for r in range(1, layer + 1):
        lb = lb + e[r:r + 1] / tot

    h = _norm_mod(x_ref[0], nw_ref[...], mod_ref[0, 0:1], mod_ref[0, 1:2])
    hb = h.astype(BF16)

    def cols(c):
        return _mm(hb, w_ref[:, c * D_MODEL:(c + 1) * D_MODEL])

    q_ref[0] = _silu(cols(0)).astype(q_ref.dtype)
    for d, f_ref in enumerate((f0_ref, f1_ref)):
        f = lb + (1.0 - lb) * jax.nn.sigmoid(cols(1 + d))
        f_ref[0] = jnp.log(f)
    v_ref[0] = cols(3).astype(v_ref.dtype)
    z_ref[0] = cols(4).astype(z_ref.dtype)


def _hg_proj(x, nw, mods, w, lb_logits, layer):
    B, L, d = x.shape
    tm = _token_tile(L)
    spec = pl.BlockSpec((1, tm, d), lambda b, t: (b, t, 0))
    sds = lambda dt: jax.ShapeDtypeStruct((B, L, d), dt)
    return pl.pallas_call(
        functools.partial(_hg_proj_kernel, layer=layer),
        grid=(B, L // tm),
        in_specs=_proj_in_specs(tm, d) + [_full(w.shape), _full(lb_logits.shape)],
        out_specs=[spec] * 5,
        out_shape=[sds(BF16), sds(F32), sds(F32), sds(BF16), sds(BF16)],
        compiler_params=_cparams("parallel", "parallel"),
        name="hg_proj",
    )(x, nw, mods, w, lb_logits)


def _post_kernel(x_ref, y_ref, mod_ref, nw_ref, wo_ref, w1_ref, w3_ref, w2_ref, o_ref, acc_ref):
    x1 = x_ref[0] + mod_ref[0, 2:3] * _mm(y_ref[0], wo_ref[...])
    hb = _norm_mod(x1, nw_ref[...], mod_ref[0, 3:4], mod_ref[0, 4:5]).astype(BF16)
    hidden = w1_ref.shape[1]
    for j in range(hidden // FFN_COLS):
        cs = slice(j * FFN_COLS, (j + 1) * FFN_COLS)
        a = (_silu(_mm(hb, w1_ref[:, cs])) * _mm(hb, w3_ref[:, cs])).astype(BF16)
        part = _mm(a, w2_ref[cs, :])
        if j == 0:
            acc_ref[...] = part
        else:
            acc_ref[...] += part
    o_ref[0] = x1 + mod_ref[0, 5:6] * acc_ref[...]


def _post(x, y, mods, nw, wo, w1, w3, w2):
    B, L, d = x.shape
    tm = _token_tile(L)
    tile = pl.BlockSpec((1, tm, d), lambda b, t: (b, t, 0))

    def resident(shape):
        return pl.BlockSpec(shape, lambda b, t: (0, 0), pipeline_mode=pl.Buffered(1))

    return pl.pallas_call(
        _post_kernel,
        grid=(B, L // tm),
        in_specs=[tile, tile,
                  pl.BlockSpec((1, 6, d), lambda b, t: (b, 0, 0)),
                  pl.BlockSpec((1, d), lambda b, t: (0, 0)),
                  resident(wo.shape), resident(w1.shape), resident(w3.shape), resident(w2.shape)],
        out_specs=tile,
        out_shape=jax.ShapeDtypeStruct((B, L, d), F32),
        scratch_shapes=[pltpu.VMEM((tm, d), F32)],
        compiler_params=_cparams("parallel", "parallel"),
        name="post_ffn",
    )(x, y, mods, nw, wo, w1, w3, w2)


def _final_norm_kernel(x_ref, nw_ref, o_ref):
    x = x_ref[0]
    ms = jnp.mean(x * x, axis=-1, keepdims=True)
    o_ref[0] = x * lax.rsqrt(ms + EPS) * nw_ref[...]


def _final_norm(x, nw):
    B, L, d = x.shape
    tm = _token_tile(L)
    tile = pl.BlockSpec((1, tm, d), lambda b, t: (b, t, 0))
    return pl.pallas_call(
        _final_norm_kernel,
        grid=(B, L // tm),
        in_specs=[tile, pl.BlockSpec((1, d), lambda b, t: (0, 0))],
        out_specs=tile,
        out_shape=jax.ShapeDtypeStruct((B, L, d), F32),
        compiler_params=_cparams("parallel", "parallel"),
        name="final_norm",
    )(x, nw)


GDN_GROUP = 8


def _block_diag(m):
    left = lax.broadcasted_iota(jnp.int32, m.shape, 1) < CHUNK
    return jnp.concatenate([jnp.where(left, m, 0.0), jnp.where(left, 0.0, m)], axis=0).astype(BF16)


def _dup_diag(m16):
    z = jnp.zeros_like(m16)
    return jnp.concatenate([jnp.concatenate([m16, z], axis=1),
                            jnp.concatenate([z, m16], axis=1)], axis=0)


def _packed_masks():
    ii = lax.broadcasted_iota(jnp.int32, (CHUNK, 2 * CHUNK), 0)
    jj = lax.broadcasted_iota(jnp.int32, (CHUNK, 2 * CHUNK), 1)
    jl = jj & (CHUNK - 1)
    bwd = jj >= CHUNK
    fwd = jnp.logical_not(bwd)
    incl = (fwd & (ii >= jl)) | (bwd & (ii <= jl))
    strict = (fwd & (ii > jl)) | (bwd & (ii < jl))
    same = {b: (ii & -b) == (jl & -b) for b in (4, 8, 16, 32, 64)}
    return dict(left=fwd, incl=incl, strict=strict, eye=ii == jl, same=same)


def _tri_inverse_packed(nms, masks):
    same = masks["same"]
    n4 = [jnp.where(same[4], n, 0.0) for n in nms]
    n4d = [_block_diag(a) for a in n4]
    p2 = [_mm(a.astype(BF16), d) for a, d in zip(n4, n4d)]
    p3 = [_mm(p.astype(BF16), d) for p, d in zip(p2, n4d)]
    eye = jnp.where(masks["eye"], 1.0, 0.0)
    xs = [eye + a + b + c for a, b, c in zip(n4, p2, p3)]
    b = 4
    while b < CHUNK:
        off_block = same[2 * b] & jnp.logical_not(same[b])
        cms = [jnp.where(off_block, n, 0.0).astype(BF16) for n in nms]
        xds = [_block_diag(x) for x in xs]
        yds = [_block_diag(_mm(c, d)) for c, d in zip(cms, xds)]
        xs = [x + _mm(x.astype(BF16), d) for x, d in zip(xs, yds)]
        b *= 2
    return xs


def _gdn_precompute(q_ref, k_ref, v_ref, gp_ref, cwq, cwk, cwv, base, n_chunks,
                    mp_s, cp_s, qp_s, op_s, dl_s):
    masks = _packed_masks()
    left, incl, strict = masks["left"], masks["incl"], masks["strict"]
    grp = min(GDN_GROUP, n_chunks)

    def group(t0):
        ts = [t0 + g for g in range(grp)]
        tiles = [gp_ref[0, 0, t] for t in ts]
        cols = [tl.T for tl in tiles]
        qs, ks, v16s = [], [], []
        for t in ts:
            q = _silu(_conv_chunk(q_ref, t, n_chunks, cwq))
            qs.append(q * lax.rsqrt(jnp.sum(q * q, axis=-1, keepdims=True) + EPS) * (DH ** -0.5))
            k = _silu(_conv_chunk(k_ref, t, n_chunks, cwk))
            ks.append(k * lax.rsqrt(jnp.sum(k * k, axis=-1, keepdims=True) + EPS))
            v16s.append(_silu(_conv_chunk(v_ref, t, n_chunks, cwv)).astype(BF16))
        k16s = [k.astype(BF16) for k in ks]
        q16s = [q.astype(BF16) for q in qs]
        kqs = [_mm_nt(jnp.concatenate([k, q], axis=0), jnp.concatenate([k, k], axis=0))
               for k, q in zip(k16s, q16s)]
        g_rows = [tl[0:1, :] for tl in tiles]
        b_rows = [tl[1:2, :] for tl in tiles]
        g_cols = [jnp.where(left, c[0:CHUNK, 0:1], c[CHUNK:2 * CHUNK, 0:1]) for c in cols]
        b_cols = [jnp.where(left, c[0:CHUNK, 1:2], c[CHUNK:2 * CHUNK, 1:2]) for c in cols]
        decays = [jnp.where(incl, jnp.exp(jnp.where(incl, gc - gr, 0.0)), 0.0)
                  for gc, gr in zip(g_cols, g_rows)]
        nms = [jnp.where(strict, -(kq[0:CHUNK] * bc * dc), 0.0)
               for kq, bc, dc in zip(kqs, b_cols, decays)]
        tinvs = _tri_inverse_packed(nms, masks)
        tus = [(x * br).astype(BF16) for x, br in zip(tinvs, b_rows)]
        tws = [(x * (br * jnp.exp(gr))).astype(BF16) for x, br, gr in zip(tinvs, b_rows, g_rows)]
        uus = [_mm(tu, _dup_diag(v)) for tu, v in zip(tus, v16s)]
        wws = [_mm(tw, _dup_diag(k)) for tw, k in zip(tws, k16s)]
        u16s = [u.astype(BF16) for u in uus]
        w16s = [w.astype(BF16) for w in wws]
        a16s = [jnp.where(incl, kq[CHUNK:2 * CHUNK] * dc, 0.0).astype(BF16)
                for kq, dc in zip(kqs, decays)]
        zero = jnp.zeros((CHUNK, DH), BF16)
        aus = [_mm(a, jnp.concatenate(
            [jnp.concatenate([w[:, 0:DH], u[:, 0:DH], zero, zero], axis=1),
             jnp.concatenate([zero, zero, w[:, DH:2 * DH], u[:, DH:2 * DH]], axis=1)], axis=0))
            for a, w, u in zip(a16s, w16s, u16s)]
        g_tots = [(tl[0:1, CHUNK - 1:CHUNK], tl[0:1, CHUNK:CHUNK + 1]) for tl in tiles]
        g_colds = [(c[0:CHUNK, 0:1], c[CHUNK:2 * CHUNK, 0:1]) for c in cols]
        kes = [[(k * jnp.exp(gt[d] - gc[d])).astype(BF16) for d in range(2)]
               for k, gt, gc in zip(ks, g_tots, g_colds)]
        kwus = [[_mm_tn(ke[d], jnp.concatenate([w[:, d * DH:(d + 1) * DH],
                                                u[:, d * DH:(d + 1) * DH]], axis=1))
                 for d in range(2)] for ke, w, u in zip(kes, w16s, u16s)]
        for g, t in enumerate(ts):
            dst = _rows(t, base)
            ci = base // CHUNK + t
            dst2 = pl.ds(ci * DH, DH) if isinstance(ci, int) else \
                pl.ds(pl.multiple_of(ci * DH, DH), DH)
            for d in range(2):
                qe = qs[g] * jnp.exp(g_colds[g][d])
                qp_s[d, dst, :] = (qe - aus[g][:, 2 * d * DH:(2 * d + 1) * DH]).astype(BF16)
                op_s[d, dst, :] = aus[g][:, (2 * d + 1) * DH:(2 * d + 2) * DH]
                mp_s[d, dst2, :] = kwus[g][d][:, 0:DH].astype(BF16)
                cp_s[d, dst2, :] = kwus[g][d][:, DH:2 * DH].astype(BF16)
                dl_s[d, pl.ds(ci, 1), :] = jnp.broadcast_to(jnp.exp(g_tots[g][d]), (1, DH))

    n_groups = n_chunks // grp
    group(0)
    if n_groups > 2:
        def body(i, carry):
            group(i * grp)
            return carry
        lax.fori_loop(1, n_groups - 1, body, 0)
    if n_groups > 1:
        group((n_groups - 1) * grp)


def _gdn_mixer_kernel(qc, kc, vc, zc, gpc, ql, kl, vl, zl, gpl, cwq, cwk, cwv, nw, yc, yl,
                      mp_s, cp_s, qp_s, op_s, dl_s, oh, sf, sb, *, Lc, Ll):
    nc, nl = Lc // CHUNK, Ll // CHUNK
    cwq_v, cwk_v, cwv_v = cwq[...], cwk[...], cwv[...]
    scr = (mp_s, cp_s, qp_s, op_s, dl_s)
    _gdn_precompute(qc, kc, vc, gpc, cwq_v, cwk_v, cwv_v, 0, nc, *scr)
    _gdn_precompute(ql, kl, vl, gpl, cwq_v, cwk_v, cwv_v, Lc, nl, *scr)
    sf[...] = jnp.zeros_like(sf)
    sb[...] = jnp.zeros_like(sb)
    nw_v = nw[...]
    s_refs = (sf, sb)

    def scan(z_ref, y_ref, base, n):
        def steps(cs):
            rbs = [_rows(c, base) for c in cs]
            cis = [base // CHUNK + c for c in cs]
            rms = [pl.ds(pl.multiple_of(ci * DH, DH), DH) for ci in cis]
            ss = [r[...] for r in s_refs]
            res = [_mm(jnp.concatenate([mp_s[d, rms[d], :], qp_s[d, rbs[d], :]], axis=0),
                       ss[d].astype(BF16)) for d in range(2)]
            outs = [res[d][DH:DH + CHUNK] + op_s[d, rbs[d], :] for d in range(2)]
            for d in range(2):
                dl = dl_s[d, pl.ds(cis[d], 1), :]
                s_refs[d][...] = ss[d] * dl - res[d][0:DH] + cp_s[d, rms[d], :].astype(F32)
            return rbs, outs

        def first_visit(cs):
            rbs, outs = steps(cs)
            for d in range(2):
                oh[rbs[d], :] = outs[d]

        def second_visit(cs):
            rbs, outs = steps(cs)
            for d in range(2):
                o = outs[d] + oh[rbs[d], :]
                y = o * lax.rsqrt(jnp.mean(o * o, axis=-1, keepdims=True) + EPS) * nw_v
                r = _rows(cs[d])
                y_ref[0, r, :] = (y * _silu(z_ref[0, r, :].astype(F32))).astype(y_ref.dtype)

        def loop(lo, hi, visit):
            def body(i, carry):
                visit((i, n - 1 - i))
                return carry
            lax.fori_loop(lo, hi, body, 0)

        loop(0, n // 2, first_visit)
        loop(n // 2, n, second_visit)

    scan(zc, yc, 0, nc)
    scan(zl, yl, Lc, nl)


def _gdn_mixer(pc, grc, plat, grl, conv_w, norm_w):
    B, Lc, _ = pc.shape
    Ll = plat.shape[1]

    def pack_rows(gr, L):
        g = gr.reshape(B, 2, 2, HEADS, L // CHUNK, CHUNK).transpose(0, 3, 4, 1, 2, 5)
        g = g.reshape(B, HEADS, L // CHUNK, 2, 2 * CHUNK)
        return jnp.pad(g, ((0, 0), (0, 0), (0, 0), (0, 6), (0, 0)))

    def seq_specs(L, mode):
        head_blk = lambda k: pl.BlockSpec((1, L, DH), lambda b, h, k=k: (b, 0, k * HEADS + h),
                                          pipeline_mode=mode)
        return [head_blk(0), head_blk(1), head_blk(2), head_blk(3),
                pl.BlockSpec((1, 1, L // CHUNK, 8, 2 * CHUNK), lambda b, h: (b, h, 0, 0, 0))]

    cw_blk = lambda k: pl.BlockSpec((4, DH), lambda b, h, k=k: (0, k * HEADS + h))
    y_spec = lambda L: pl.BlockSpec((1, L, DH), lambda b, h: (b, 0, h))
    Lt = Lc + Ll
    per_dir = lambda w, dt: pltpu.VMEM((2, Lt, w), dt)
    return pl.pallas_call(
        functools.partial(_gdn_mixer_kernel, Lc=Lc, Ll=Ll),
        grid=(B, HEADS),
        in_specs=seq_specs(Lc, None) + seq_specs(Ll, pl.Buffered(1)) + [
            cw_blk(0), cw_blk(1), cw_blk(2), pl.BlockSpec((1, DH), lambda b, h: (0, 0))],
        out_specs=[y_spec(Lc), y_spec(Ll)],
        out_shape=[jax.ShapeDtypeStruct((B, Lc, HEADS * DH), BF16),
                   jax.ShapeDtypeStruct((B, Ll, HEADS * DH), BF16)],
        scratch_shapes=[pltpu.VMEM((2, Lt // CHUNK * DH, DH), BF16),
                        pltpu.VMEM((2, Lt // CHUNK * DH, DH), BF16),
                        per_dir(DH, BF16), per_dir(DH, F32),
                        pltpu.VMEM((2, Lt // CHUNK, DH), F32),
                        pltpu.VMEM((Lt, DH), F32), pltpu.VMEM((DH, DH), F32),
                        pltpu.VMEM((DH, DH), F32)],
        compiler_params=_cparams("parallel", "arbitrary"),
        name="gdn_mixer",
    )(pc, pc, pc, pc, pack_rows(grc, Lc), plat, plat, plat, plat, pack_rows(grl, Ll),
      conv_w, conv_w, conv_w, norm_w)


SCAN_GROUP = 4


def _hg_chunk_parts(q16, logf, v16, rev):
    rows = lax.broadcasted_iota(jnp.int32, (CHUNK, DH), 0)
    gcum = _chunk_cumsum(logf, rows, rev)
    mid = CHUNK // 2 - 1 if rev else CHUNK // 2
    end = 0 if rev else CHUNK - 1
    ref = gcum[mid:mid + 1, :]
    glast = gcum[end:end + 1, :]
    q = q16.astype(F32)
    k = 1.0 - jnp.exp(logf)
    ii = lax.broadcasted_iota(jnp.int32, (CHUNK, CHUNK), 0)
    jj = lax.broadcasted_iota(jnp.int32, (CHUNK, CHUNK), 1)
    incl = (ii <= jj) if rev else (ii >= jj)
    scores = _mm_nt((q * jnp.exp(gcum - ref)).astype(BF16), (k * jnp.exp(ref - gcum)).astype(BF16))
    intra = _mm(jnp.where(incl, scores, 0.0).astype(BF16), v16)
    qd = (q * jnp.exp(gcum)).astype(BF16)
    upd = _mm_tn(v16, (k * jnp.exp(glast - gcum)).astype(BF16))
    return intra, qd, upd, jnp.exp(glast)


def _hg_mixer_kernel(qc, f0c, f1c, vc, zc, ql, f0l, f1l, vl, zl, nw, yc, yl, oh, sf, sb,
                     *, Lc, Ll):
    nc, nl = Lc // CHUNK, Ll // CHUNK
    sf[...] = jnp.zeros_like(sf)
    sb[...] = jnp.zeros_like(sb)
    nw_v = nw[...]

    def scan(q_ref, f_refs, v_ref, z_ref, y_ref, base, n):
        grp = min(SCAN_GROUP, n)
        n_groups = n // grp

        def advance(i):
            res = []
            for d, s_ref in enumerate((sf, sb)):
                cs = [i * grp + g for g in range(grp)] if d == 0 else \
                    [n - 1 - i * grp - g for g in range(grp)]
                parts = [_hg_chunk_parts(q_ref[0, _rows(c), :], f_refs[d][0, _rows(c), :],
                                         v_ref[0, _rows(c), :], d == 1) for c in cs]
                st = s_ref[...]
                outs = []
                for intra, qd, upd, dl in parts:
                    outs.append(intra + _mm_nt(qd, st.astype(BF16)))
                    st = st * dl + upd
                s_ref[...] = st
                res.append((cs, outs))
            return res

        def finish(c, o):
            y = o * lax.rsqrt(jnp.mean(o * o, axis=-1, keepdims=True) + EPS) * nw_v
            r = _rows(c)
            y_ref[0, r, :] = (y * _silu(z_ref[0, r, :].astype(F32))).astype(y_ref.dtype)

        def rb(c):
            return _rows(c, base)

        def first_visit(i, carry):
            for cs, outs in advance(i):
                for c, o in zip(cs, outs):
                    oh[rb(c), :] = o
            return carry

        def second_visit(i, carry):
            for cs, outs in advance(i):
                for c, o in zip(cs, outs):
                    finish(c, o + oh[rb(c), :])
            return carry

        half = n_groups // 2
        lax.fori_loop(0, half, first_visit, 0)
        if n_groups % 2:
            (cs_f, outs_f), (cs_b, outs_b) = advance(half)
            for c, o, ob in zip(cs_f, outs_f, reversed(outs_b)):
                finish(c, o + ob)
        lax.fori_loop(n_groups - half, n_groups, second_visit, 0)

    scan(qc, (f0c, f1c), vc, zc, yc, 0, nc)
    scan(ql, (f0l, f1l), vl, zl, yl, Lc, nl)


def _hg_mixer(ctx_parts, lat_parts, norm_w):
    B, Lc, _ = ctx_parts[0].shape
    Ll = lat_parts[0].shape[1]
    blk = lambda L: pl.BlockSpec((1, L, DH), lambda b, h: (b, 0, h))
    Lt = Lc + Ll
    return pl.pallas_call(
        functools.partial(_hg_mixer_kernel, Lc=Lc, Ll=Ll),
        grid=(B, HEADS),
        in_specs=[blk(Lc)] * 5 + [blk(Ll)] * 5 + [pl.BlockSpec((1, DH), lambda b, h: (0, 0))],
        out_specs=[blk(Lc), blk(Ll)],
        out_shape=[jax.ShapeDtypeStruct((B, Lc, HEADS * DH), BF16),
                   jax.ShapeDtypeStruct((B, Ll, HEADS * DH), BF16)],
        scratch_shapes=[pltpu.VMEM((Lt, DH), F32)] + [pltpu.VMEM((DH, DH), F32)] * 2,
        compiler_params=_cparams("parallel", "arbitrary"),
        name="hg_mixer",
    )(*ctx_parts, *lat_parts, norm_w)


def _lru_chunk_parts(x, w_ref, b_ref, sp, d):
    rev = d == 1
    cs = slice(2 * DH * d, 2 * DH * (d + 1))
    gates = _mm(x.astype(BF16), w_ref[0, :, cs]) + b_ref[0, :, cs]
    r = jax.nn.sigmoid(gates[:, 0:DH])
    ig = jax.nn.sigmoid(gates[:, DH:2 * DH])
    log_a = -LRU_C * r * sp
    a = jnp.exp(log_a)
    bv = jnp.sqrt(1.0 - jnp.exp(2.0 * log_a)) * (ig * x)
    rows = lax.broadcasted_iota(jnp.int32, (CHUNK, DH), 0)
    k = 1
    while k < CHUNK:
        if rev:
            m = rows < CHUNK - k
            a_sh, b_sh = _shift_rows(a, k), _shift_rows(bv, k)
        else:
            m = rows >= k
            a_sh, b_sh = _shift_rows(a, -k), _shift_rows(bv, -k)
        bv = jnp.where(m, bv + a * b_sh, bv)
        a = jnp.where(m, a * a_sh, a)
        k *= 2
    return a, bv


def _lru_mixer_kernel(xc, gc, xl, gl, cw, cb, w4, b4, lam, yc, yl, xs, of, ob, *, Lc, Ll):
    nc, nl = Lc // CHUNK, Ll // CHUNK
    cw_v = cw[...]
    cb_v = cb[...]

    def prep(x_ref, base, n):
        def fn(t):
            dst = pl.ds(base + t * CHUNK, CHUNK) if isinstance(t, int) else \
                pl.ds(pl.multiple_of(base + t * CHUNK, CHUNK), CHUNK)
            xs[dst, :] = _conv_chunk(x_ref, t, n, cw_v) + cb_v
        _for_chunks(n, fn)

    prep(xc, 0, nc)
    prep(xl, Lc, nl)
    sp0 = _softplus(-lam[0:1, :])
    sp1 = _softplus(-lam[1:2, :])

    def scan(base, n, carry):
        grp = min(SCAN_GROUP, n)

        def body(i, c):
            hs = list(c)
            for d, (sp, o_ref) in enumerate(((sp0, of), (sp1, ob))):
                cs = [i * grp + g for g in range(grp)] if d == 0 else \
                    [n - 1 - i * grp - g for g in range(grp)]
                parts = [_lru_chunk_parts(xs[_rows(t, base), :], w4, b4, sp, d) for t in cs]
                end = CHUNK - 1 if d == 0 else 0
                for t, (a_cum, b_cum) in zip(cs, parts):
                    hrows = b_cum + a_cum * hs[d]
                    o_ref[_rows(t, base), :] = hrows
                    hs[d] = hrows[end:end + 1, :]
            return tuple(hs)
        return lax.fori_loop(0, n // grp, body, carry)

    zero = jnp.zeros((1, DH), F32)
    carry = scan(0, nc, (zero, zero))
    scan(Lc, nl, carry)

    def out(g_ref, y_ref, base, n):
        def body(t, c):
            r = _rows(t)
            rb = pl.ds(pl.multiple_of(base + t * CHUNK, CHUNK), CHUNK)
            y_ref[0, r, :] = (g_ref[0, r, :].astype(F32) * (of[rb, :] + ob[rb, :])).astype(y_ref.dtype)
            return c
        lax.fori_loop(0, n, body, 0)

    out(gc, yc, 0, nc)
    out(gl, yl, Lc, nl)


def _lru_mixer(xr_c, gate_c, xr_l, gate_l, conv_w, conv_b, w4, b4, lam):
    B, Lc, _ = xr_c.shape
    Ll = xr_l.shape[1]
    blk = lambda L: pl.BlockSpec((1, L, DH), lambda b, g: (b, 0, g))
    Lt = Lc + Ll
    return pl.pallas_call(
        functools.partial(_lru_mixer_kernel, Lc=Lc, Ll=Ll),
        grid=(B, HEADS),
        in_specs=[blk(Lc), blk(Lc), blk(Ll), blk(Ll),
                  pl.BlockSpec((4, DH), lambda b, g: (0, g)),
                  pl.BlockSpec((1, DH), lambda b, g: (0, g)),
                  pl.BlockSpec((1, DH, 4 * DH), lambda b, g: (g, 0, 0)),
                  pl.BlockSpec((1, 1, 4 * DH), lambda b, g: (g, 0, 0)),
                  pl.BlockSpec((2, DH), lambda b, g: (0, g))],
        out_specs=[blk(Lc), blk(Ll)],
        out_shape=[jax.ShapeDtypeStruct((B, Lc, HEADS * DH), BF16),
                   jax.ShapeDtypeStruct((B, Ll, HEADS * DH), BF16)],
        scratch_shapes=[pltpu.VMEM((Lt, DH), F32)] * 3,
        compiler_params=_cparams("parallel", "arbitrary"),
        name="lru_mixer",
    )(xr_c, gate_c, xr_l, gate_l, conv_w, conv_b, w4, b4, lam)


def _to_scan_order(h):
    bsz, L, d = h.shape
    return h.reshape(bsz, L // GRID_W, GRID_W, d).transpose(0, 2, 1, 3).reshape(bsz, L, d)


def _from_scan_order(h):
    bsz, L, d = h.shape
    return h.reshape(bsz, GRID_W, L // GRID_W, d).transpose(0, 2, 1, 3).reshape(bsz, L, d)


def _pad_lanes(v, layout):
    row = jnp.zeros((DH,), F32)
    for off in layout:
        row = row.at[off:off + v.shape[0]].set(v)
    return row.reshape(1, DH)


def kernel(x, c, ctx, c_ctx, ada_w, ada_b, norm_mix, norm_ffn, norm_final, ffn_w1, ffn_w3, ffn_w2, gdn_w_in, gdn_conv, gdn_a_log, gdn_dt_bias, gdn_norm, gdn_w_out, lru_w_in, lru_conv_w, lru_conv_b, lru_w_r, lru_b_r, lru_w_i, lru_b_i, lru_lambda, lru_w_out, hg_w_in, hg_lb_logits, hg_norm, hg_w_out):
    B, L, d = x.shape
    depth = ada_w.shape[0]
    assert d == D_MODEL and L % GRID_W == 0 and L % CHUNK == 0 and ctx.shape[1] % CHUNK == 0
    assert B + 1 <= 8

    cvec = jnp.zeros((8, d), F32).at[0:B].set(c).at[B].set(c_ctx)
    mods = _mods(cvec, ada_w, ada_b).reshape(depth, 8, 6, d)

    xl, xc = x, ctx
    for i in range(depth):
        mod_l = mods[i, 0:B]
        mod_c = jnp.broadcast_to(mods[i, B][None], (B, 6, d))
        col = i % 2 == 1
        if col:
            xl = _to_scan_order(xl)
        nw = norm_mix[i].reshape(1, d)
        kind, j = i % 3, i // 3
        if kind == 0:
            w_in = gdn_w_in[j]
            n_main = 4 * HEADS * DH
            w_main = w_in[:, :n_main].astype(BF16)
            wa = w_in[:, n_main:n_main + 2 * HEADS]
            wb = w_in[:, n_main + 2 * HEADS:n_main + 4 * HEADS]
            wab = jnp.zeros((d, DH), F32)
            wab = wab.at[:, 0:2 * HEADS].set(wa).at[:, 2 * HEADS:4 * HEADS].set(wb)
            wab_hi = wab.astype(BF16)
            wab_lo = (wab - wab_hi.astype(F32)).astype(BF16)
            alog = _pad_lanes(gdn_a_log[j].reshape(-1), (0,))
            dtb = _pad_lanes(gdn_dt_bias[j].reshape(-1), (0,))
            pc, grc = _gdn_proj(xc, nw, mod_c, w_main, wab_hi, wab_lo, alog, dtb)
            plat, grl = _gdn_proj(xl, nw, mod_l, w_main, wab_hi, wab_lo, alog, dtb)
            yc, yl = _gdn_mixer(pc, grc, plat, grl, gdn_conv[j], gdn_norm[j].reshape(1, DH))
            w_out = gdn_w_out[j]
        elif kind == 1:
            w_in = lru_w_in[j].astype(BF16)
            gate_c, xr_c = _lru_proj(xc, nw, mod_c, w_in)
            gate_l, xr_l = _lru_proj(xl, nw, mod_l, w_in)
            w4 = jnp.concatenate([lru_w_r[j, 0], lru_w_i[j, 0], lru_w_r[j, 1], lru_w_i[j, 1]],
                                 axis=-1).astype(BF16)
            b4 = jnp.stack([lru_b_r[j, 0], lru_b_i[j, 0], lru_b_r[j, 1], lru_b_i[j, 1]], axis=0)
            b4 = b4.reshape(4, HEADS, DH).transpose(1, 0, 2).reshape(HEADS, 1, 4 * DH)
            yc, yl = _lru_mixer(xr_c, gate_c, xr_l, gate_l, lru_conv_w[j],
                                lru_conv_b[j].reshape(1, d), w4, b4, lru_lambda[j])
            w_out = lru_w_out[j]
        else:
            w_in = hg_w_in[j].astype(BF16)
            parts_c = _hg_proj(xc, nw, mod_c, w_in, hg_lb_logits, i)
            parts_l = _hg_proj(xl, nw, mod_l, w_in, hg_lb_logits, i)
            yc, yl = _hg_mixer(parts_c, parts_l, hg_norm[j].reshape(1, DH))
            w_out = hg_w_out[j]

        nwf = norm_ffn[i].reshape(1, d)
        wo, w1, w3, w2 = (w.astype(BF16) for w in (w_out, ffn_w1[i], ffn_w3[i], ffn_w2[i]))
        xl = _post(xl, yl, mod_l, nwf, wo, w1, w3, w2)
        if i != depth - 1:
            xc = _post(xc, yc, mod_c, nwf, wo, w1, w3, w2)
        if col:
            xl = _from_scan_order(xl)
    return _final_norm(xl, norm_final.reshape(1, d))
```

```python
import functools
import math

import jax
import jax.numpy as jnp
from jax import lax
from jax.experimental import pallas as pl
from jax.experimental.pallas import tpu as pltpu

F32 = jnp.float32
BF16 = jnp.bfloat16

D_MODEL = 1024
HEADS = 8
DH = 128
CHUNK = 64
GDN_CHUNK = 128
GRID_W = 64
EPS = 1e-6
LRU_C = 8.0
FFN_HIDDEN = 2816
FFN_COLS = 256
TOKEN_TILE = 512
V7X_VMEM_BYTES = 64 * 1024 * 1024
VMEM_LIMIT = V7X_VMEM_BYTES - 6 * 1024 * 1024


def _cparams(*sem):
    return pltpu.CompilerParams(dimension_semantics=sem, vmem_limit_bytes=VMEM_LIMIT)


def _mm(a, b):
    return jnp.dot(a, b, preferred_element_type=F32)


def _mm_nt(a, b):
    return lax.dot_general(a, b, (((1,), (1,)), ((), ())), preferred_element_type=F32)


def _mm_tn(a, b):
    return lax.dot_general(a, b, (((0,), (0,)), ((), ())), preferred_element_type=F32)


def _silu(x):
    return x * jax.nn.sigmoid(x)


def _softplus(x):
    return jnp.maximum(x, 0.0) + jnp.log(1.0 + jnp.exp(-jnp.abs(x)))


def _gelu_tanh(x):
    return 0.5 * x * (1.0 + jnp.tanh(math.sqrt(2.0 / math.pi) * (x + 0.044715 * (x * x * x))))


def _norm_mod(x, nw, shift, scale):
    ms = jnp.mean(x * x, axis=-1, keepdims=True)
    y = x * lax.rsqrt(ms + EPS) * nw
    return y * (1.0 + scale) + shift


def _shift_rows(x, k):
    n = x.shape[0]
    return x if k % n == 0 else pltpu.roll(x, (-k) % n, 0)


def _chunk_cumsum(x, rows_in_chunk, rev, chunk=CHUNK):
    k = 1
    while k < chunk:
        if rev:
            x = x + jnp.where(rows_in_chunk < chunk - k, _shift_rows(x, k), 0.0)
        else:
            x = x + jnp.where(rows_in_chunk >= k, _shift_rows(x, -k), 0.0)
        k *= 2
    return x


def _conv_chunk(ref, t, n_chunks, cw, chunk=CHUNK):
    first = isinstance(t, int) and t == 0
    last = isinstance(t, int) and t == n_chunks - 1
    halo = 16
    if first:
        win = ref[0, 0:chunk + halo, :].astype(F32)
        off = 0
    elif last:
        start = t * chunk - halo
        win = ref[0, start:start + chunk + halo, :].astype(F32)
        off = halo
    else:
        start = t * chunk - halo
        if not isinstance(t, int):
            start = pl.multiple_of(start, halo)
        win = ref[0, pl.ds(start, chunk + 2 * halo), :].astype(F32)
        off = halo
    rows = lax.broadcasted_iota(jnp.int32, (chunk, win.shape[1]), 0)

    def tap(k):
        return _shift_rows(win, k)[off:off + chunk]

    xm2, xm1, x0, xp1 = tap(-2), tap(-1), tap(0), tap(1)
    if first:
        xm2 = jnp.where(rows >= 2, xm2, 0.0)
        xm1 = jnp.where(rows >= 1, xm1, 0.0)
    if last:
        xp1 = jnp.where(rows < chunk - 1, xp1, 0.0)
    return cw[0:1] * xm2 + cw[1:2] * xm1 + cw[2:3] * x0 + cw[3:4] * xp1


def _for_chunks(n_chunks, fn):
    fn(0)
    if n_chunks > 2:
        def body(t, carry):
            fn(t)
            return carry
        lax.fori_loop(1, n_chunks - 1, body, 0)
    if n_chunks > 1:
        fn(n_chunks - 1)


def _rows(t, base=0, chunk=CHUNK):
    start = base + t * chunk
    if not isinstance(start, int):
        start = pl.multiple_of(start, chunk)
    return pl.ds(start, chunk)


def _gated_norm_out(of, ob, z_ref, y_ref, nw, base, n_chunks):
    def body(t, carry):
        r = _rows(t)
        rb = pl.ds(pl.multiple_of(base + t * CHUNK, CHUNK), CHUNK)
        o = of[rb, :] + ob[rb, :]
        ms = jnp.mean(o * o, axis=-1, keepdims=True)
        y = o * lax.rsqrt(ms + EPS) * nw
        y_ref[0, r, :] = (y * _silu(z_ref[0, r, :].astype(F32))).astype(y_ref.dtype)
        return carry
    lax.fori_loop(0, n_chunks, body, 0)


def _mods_kernel(cv_ref, w_ref, b_ref, o_ref):
    s = _silu(cv_ref[...])
    o_ref[0] = jnp.dot(s, w_ref[0], preferred_element_type=F32,
                       precision=lax.Precision.HIGHEST) + b_ref[0]


def _mods(cvec, ada_w, ada_b):
    depth, d, n = ada_w.shape
    tn = 1536
    return pl.pallas_call(
        _mods_kernel,
        grid=(depth, n // tn),
        in_specs=[
            pl.BlockSpec((8, d), lambda i, j: (0, 0)),
            pl.BlockSpec((1, d, tn), lambda i, j: (i, 0, j)),
            pl.BlockSpec((1, 1, tn), lambda i, j: (i, 0, j)),
        ],
        out_specs=pl.BlockSpec((1, 8, tn), lambda i, j: (i, 0, j)),
        out_shape=jax.ShapeDtypeStruct((depth, 8, n), F32),
        compiler_params=_cparams("parallel", "parallel"),
        name="adaln_mods",
    )(cvec, ada_w, ada_b.reshape(depth, 1, n))


def _token_tile(L):
    return min(TOKEN_TILE, L)


def _proj_in_specs(tm, d):
    return [
        pl.BlockSpec((1, tm, d), lambda b, t: (b, t, 0)),
        pl.BlockSpec((1, d), lambda b, t: (0, 0)),
        pl.BlockSpec((1, 6, d), lambda b, t: (b, 0, 0)),
    ]


def _full(shape):
    nd = len(shape)
    return pl.BlockSpec(shape, lambda b, t: (0,) * nd)


def _gdn_proj_kernel(x_ref, nw_ref, mod_ref, w_ref, wabh_ref, wabl_ref, alog_ref, dtb_ref,
                     qkvz_ref, grow_ref):
    h = _norm_mod(x_ref[0], nw_ref[...], mod_ref[0, 0:1], mod_ref[0, 1:2])
    hb = h.astype(BF16)
    n = w_ref.shape[1]
    for c in range(n // D_MODEL):
        cs = slice(c * D_MODEL, (c + 1) * D_MODEL)
        qkvz_ref[0, :, cs] = _mm(hb, w_ref[:, cs]).astype(qkvz_ref.dtype)
    hl = (h - hb.astype(F32)).astype(BF16)
    ab = _mm(hb, wabh_ref[...]) + _mm(hl, wabh_ref[...]) + _mm(hb, wabl_ref[...])
    lane = lax.broadcasted_iota(jnp.int32, ab.shape, 1)
    rin = lax.broadcasted_iota(jnp.int32, ab.shape, 0) & (GDN_CHUNK - 1)
    g = -jnp.exp(alog_ref[...]) * _softplus(ab + dtb_ref[...])
    res = jnp.where(lane < HEADS, _chunk_cumsum(g, rin, False, GDN_CHUNK),
                    jnp.where(lane < 2 * HEADS, _chunk_cumsum(g, rin, True, GDN_CHUNK),
                              jax.nn.sigmoid(ab)))
    grow_ref[0] = res.T[0:4 * HEADS, :]


def _gdn_proj(x, nw, mods, w_main, wab_hi, wab_lo, alog, dtb):
    B, L, d = x.shape
    tm = _token_tile(L)
    n = w_main.shape[1]
    return pl.pallas_call(
        _gdn_proj_kernel,
        grid=(B, L // tm),
        in_specs=_proj_in_specs(tm, d) + [
            _full(w_main.shape), _full(wab_hi.shape), _full(wab_lo.shape),
            _full(alog.shape), _full(dtb.shape)],
        out_specs=[
            pl.BlockSpec((1, tm, n), lambda b, t: (b, t, 0)),
            pl.BlockSpec((1, 4 * HEADS, tm), lambda b, t: (b, 0, t)),
        ],
        out_shape=[
            jax.ShapeDtypeStruct((B, L, n), BF16),
            jax.ShapeDtypeStruct((B, 4 * HEADS, L), F32),
        ],
        compiler_params=_cparams("parallel", "parallel"),
        name="gdn_proj",
    )(x, nw, mods, w_main, wab_hi, wab_lo, alog, dtb)


def _lru_proj_kernel(x_ref, nw_ref, mod_ref, w_ref, gate_ref, xr_ref):
    h = _norm_mod(x_ref[0], nw_ref[...], mod_ref[0, 0:1], mod_ref[0, 1:2])
    hb = h.astype(BF16)
    gate_ref[0] = _gelu_tanh(_mm(hb, w_ref[:, 0:D_MODEL])).astype(gate_ref.dtype)
    xr_ref[0] = _mm(hb, w_ref[:, D_MODEL:2 * D_MODEL])


def _lru_proj(x, nw, mods, w):
    B, L, d = x.shape
    tm = _token_tile(L)
    spec = pl.BlockSpec((1, tm, d), lambda b, t: (b, t, 0))
    return pl.pallas_call(
        _lru_proj_kernel,
        grid=(B, L // tm),
        in_specs=_proj_in_specs(tm, d) + [_full(w.shape)],
        out_specs=[spec, spec],
        out_shape=[jax.ShapeDtypeStruct((B, L, d), BF16), jax.ShapeDtypeStruct((B, L, d), F32)],
        compiler_params=_cparams("parallel", "parallel"),
        name="lru_proj",
    )(x, nw, mods, w)


def _hg_proj_kernel(x_ref, nw_ref, mod_ref, w_ref, lbl_ref, q_ref, f0_ref, f1_ref, v_ref, z_ref,
                    *, layer):
    logits = lbl_ref[...]
    depth = logits.shape[0]
    m = logits[0:1]
    for r in range(1, depth):
        m = jnp.maximum(m, logits[r:r + 1])
    e = jnp.exp(logits - m)
    tot = e[0:1]
    for r in range(1, depth):
        tot = tot + e[r:r + 1]
    lb = jnp.zeros_like(tot)
    for r in range(1, layer + 1):
        lb = lb + e[r:r + 1] / tot

    h = _norm_mod(x_ref[0], nw_ref[...], mod_ref[0, 0:1], mod_ref[0, 1:2])
    hb = h.astype(BF16)

    def cols(c):
        return _mm(hb, w_ref[:, c * D_MODEL:(c + 1) * D_MODEL])

    q_ref[0] = _silu(cols(0)).astype(q_ref.dtype)
    for d, f_ref in enumerate((f0_ref, f1_ref)):
        f = lb + (1.0 - lb) * jax.nn.sigmoid(cols(1 + d))
        f_ref[0] = jnp.log(f)
    v_ref[0] = cols(3).astype(v_ref.dtype)
    z_ref[0] = cols(4).astype(z_ref.dtype)


def _hg_proj(x, nw, mods, w, lb_logits, layer):
    B, L, d = x.shape
    tm = _token_tile(L)
    spec = pl.BlockSpec((1, tm, d), lambda b, t: (b, t, 0))
    sds = lambda dt: jax.ShapeDtypeStruct((B, L, d), dt)
    return pl.pallas_call(
        functools.partial(_hg_proj_kernel, layer=layer),
        grid=(B, L // tm),
        in_specs=_proj_in_specs(tm, d) + [_full(w.shape), _full(lb_logits.shape)],
        out_specs=[spec] * 5,
        out_shape=[sds(BF16), sds(F32), sds(F32), sds(BF16), sds(BF16)],
        compiler_params=_cparams("parallel", "parallel"),
        name="hg_proj",
    )(x, nw, mods, w, lb_logits)


def _post_kernel(x_ref, y_ref, mod_ref, nw_ref, wo_ref, w1_ref, w3_ref, w2_ref, fnw_ref,
                 o_ref, acc_ref, *, final):
    x1 = x_ref[0] + mod_ref[0, 2:3] * _mm(y_ref[0], wo_ref[...])
    hb = _norm_mod(x1, nw_ref[...], mod_ref[0, 3:4], mod_ref[0, 4:5]).astype(BF16)
    hidden = w1_ref.shape[1]
    for j in range(hidden // FFN_COLS):
        cs = slice(j * FFN_COLS, (j + 1) * FFN_COLS)
        a = (_silu(_mm(hb, w1_ref[:, cs])) * _mm(hb, w3_ref[:, cs])).astype(BF16)
        part = _mm(a, w2_ref[cs, :])
        if j == 0:
            acc_ref[...] = part
        else:
            acc_ref[...] += part
    out = x1 + mod_ref[0, 5:6] * acc_ref[...]
    if final:
        out = out * lax.rsqrt(jnp.mean(out * out, axis=-1, keepdims=True) + EPS) * fnw_ref[...]
    o_ref[0] = out


def _post(x, y, mods, nw, wo, w1, w3, w2, final_nw, final):
    B, L, d = x.shape
    tm = _token_tile(L)
    tile = pl.BlockSpec((1, tm, d), lambda b, t: (b, t, 0))

    def resident(shape):
        return pl.BlockSpec(shape, lambda b, t: (0, 0), pipeline_mode=pl.Buffered(1))

    return pl.pallas_call(
        functools.partial(_post_kernel, final=final),
        grid=(B, L // tm),
        in_specs=[tile, tile,
                  pl.BlockSpec((1, 6, d), lambda b, t: (b, 0, 0)),
                  pl.BlockSpec((1, d), lambda b, t: (0, 0)),
                  resident(wo.shape), resident(w1.shape), resident(w3.shape), resident(w2.shape),
                  pl.BlockSpec((1, d), lambda b, t: (0, 0))],
        out_specs=tile,
        out_shape=jax.ShapeDtypeStruct((B, L, d), F32),
        scratch_shapes=[pltpu.VMEM((tm, d), F32)],
        compiler_params=_cparams("parallel", "parallel"),
        name="post_ffn",
    )(x, y, mods, nw, wo, w1, w3, w2, final_nw)


GC = GDN_CHUNK
GDN_GROUP = 8


def _block_diag(m):
    left = lax.broadcasted_iota(jnp.int32, m.shape, 1) < GC
    return jnp.concatenate([jnp.where(left, m, 0.0), jnp.where(left, 0.0, m)], axis=0).astype(BF16)


def _dup_diag(m16):
    z = jnp.zeros_like(m16)
    return jnp.concatenate([jnp.concatenate([m16, z], axis=1),
                            jnp.concatenate([z, m16], axis=1)], axis=0)


def _packed_masks():
    ii = lax.broadcasted_iota(jnp.int32, (GC, 2 * GC), 0)
    jj = lax.broadcasted_iota(jnp.int32, (GC, 2 * GC), 1)
    jl = jj & (GC - 1)
    bwd = jj >= GC
    fwd = jnp.logical_not(bwd)
    incl = (fwd & (ii >= jl)) | (bwd & (ii <= jl))
    strict = (fwd & (ii > jl)) | (bwd & (ii < jl))
    same = {b: (ii & -b) == (jl & -b) for b in (4, 8, 16, 32, 64, 128) if b <= GC}
    return dict(left=fwd, incl=incl, strict=strict, eye=ii == jl, same=same)


def _tri_inverse_packed(nms, masks):
    same = masks["same"]
    n4 = [jnp.where(same[4], n, 0.0) for n in nms]
    n4d = [_block_diag(a) for a in n4]
    p2 = [_mm(a.astype(BF16), d) for a, d in zip(n4, n4d)]
    p3 = [_mm(p.astype(BF16), d) for p, d in zip(p2, n4d)]
    eye = jnp.where(masks["eye"], 1.0, 0.0)
    xs = [eye + a + b + c for a, b, c in zip(n4, p2, p3)]
    b = 4
    while b < GC:
        off_block = same[2 * b] & jnp.logical_not(same[b])
        cms = [jnp.where(off_block, n, 0.0).astype(BF16) for n in nms]
        xds = [_block_diag(x) for x in xs]
        yds = [_block_diag(_mm(c, d)) for c, d in zip(cms, xds)]
        xs = [x + _mm(x.astype(BF16), d) for x, d in zip(xs, yds)]
        b *= 2
    return xs


def _gdn_precompute(q_ref, k_ref, v_ref, gp_ref, cwq, cwk, cwv, base, n_chunks,
                    mp_s, cp_s, qp_s, op_s, dl_s):
    masks = _packed_masks()
    left, incl, strict = masks["left"], masks["incl"], masks["strict"]
    grp = min(GDN_GROUP, n_chunks)

    def group(t0):
        ts = [t0 + g for g in range(grp)]
        tiles = [gp_ref[0, 0, t] for t in ts]
        cols = [tl.T for tl in tiles]
        qs, ks, v16s = [], [], []
        for t in ts:
            q = _silu(_conv_chunk(q_ref, t, n_chunks, cwq, GC))
            qs.append(q * lax.rsqrt(jnp.sum(q * q, axis=-1, keepdims=True) + EPS) * (DH ** -0.5))
            k = _silu(_conv_chunk(k_ref, t, n_chunks, cwk, GC))
            ks.append(k * lax.rsqrt(jnp.sum(k * k, axis=-1, keepdims=True) + EPS))
            v16s.append(_silu(_conv_chunk(v_ref, t, n_chunks, cwv, GC)).astype(BF16))
        k16s = [k.astype(BF16) for k in ks]
        q16s = [q.astype(BF16) for q in qs]
        kqs = [_mm_nt(jnp.concatenate([k, q], axis=0), jnp.concatenate([k, k], axis=0))
               for k, q in zip(k16s, q16s)]
        g_rows = [tl[0:1, :] for tl in tiles]
        b_rows = [tl[1:2, :] for tl in tiles]
        g_cols = [jnp.where(left, c[0:GC, 0:1], c[GC:2 * GC, 0:1]) for c in cols]
        b_cols = [jnp.where(left, c[0:GC, 1:2], c[GC:2 * GC, 1:2]) for c in cols]
        decays = [jnp.where(incl, jnp.exp(jnp.where(incl, gc - gr, 0.0)), 0.0)
                  for gc, gr in zip(g_cols, g_rows)]
        nms = [jnp.where(strict, -(kq[0:GC] * bc * dc), 0.0)
               for kq, bc, dc in zip(kqs, b_cols, decays)]
        tinvs = _tri_inverse_packed(nms, masks)
        tus = [(x * br).astype(BF16) for x, br in zip(tinvs, b_rows)]
        tws = [(x * (br * jnp.exp(gr))).astype(BF16) for x, br, gr in zip(tinvs, b_rows, g_rows)]
        uus = [_mm(tu, _dup_diag(v)) for tu, v in zip(tus, v16s)]
        wws = [_mm(tw, _dup_diag(k)) for tw, k in zip(tws, k16s)]
        u16s = [u.astype(BF16) for u in uus]
        w16s = [w.astype(BF16) for w in wws]
        a16s = [jnp.where(incl, kq[GC:2 * GC] * dc, 0.0).astype(BF16)
                for kq, dc in zip(kqs, decays)]
        zero = jnp.zeros((GC, DH), BF16)
        aus = [_mm(a, jnp.concatenate(
            [jnp.concatenate([w[:, 0:DH], u[:, 0:DH], zero, zero], axis=1),
             jnp.concatenate([zero, zero, w[:, DH:2 * DH], u[:, DH:2 * DH]], axis=1)], axis=0))
            for a, w, u in zip(a16s, w16s, u16s)]
        g_tots = [(tl[0:1, GC - 1:GC], tl[0:1, GC:GC + 1]) for tl in tiles]
        g_colds = [(c[0:GC, 0:1], c[GC:2 * GC, 0:1]) for c in cols]
        kes = [[(k * jnp.exp(gt[d] - gc[d])).astype(BF16) for d in range(2)]
               for k, gt, gc in zip(ks, g_tots, g_colds)]
        kwus = [[_mm_tn(ke[d], jnp.concatenate([w[:, d * DH:(d + 1) * DH],
                                                u[:, d * DH:(d + 1) * DH]], axis=1))
                 for d in range(2)] for ke, w, u in zip(kes, w16s, u16s)]
        for g, t in enumerate(ts):
            dst = _rows(t, base, GC)
            ci = base // GC + t
            dst2 = pl.ds(ci * DH, DH) if isinstance(ci, int) else \
                pl.ds(pl.multiple_of(ci * DH, DH), DH)
            for d in range(2):
                qe = qs[g] * jnp.exp(g_colds[g][d])
                qp_s[d, dst, :] = (qe - aus[g][:, 2 * d * DH:(2 * d + 1) * DH]).astype(BF16)
                op_s[d, dst, :] = aus[g][:, (2 * d + 1) * DH:(2 * d + 2) * DH]
                mp_s[d, dst2, :] = kwus[g][d][:, 0:DH].astype(BF16)
                cp_s[d, dst2, :] = kwus[g][d][:, DH:2 * DH].astype(BF16)
                dl_s[d, pl.ds(ci, 1), :] = jnp.broadcast_to(jnp.exp(g_tots[g][d]), (1, DH))

    n_groups = n_chunks // grp
    group(0)
    if n_groups > 2:
        def body(i, carry):
            group(i * grp)
            return carry
        lax.fori_loop(1, n_groups - 1, body, 0)
    if n_groups > 1:
        group((n_groups - 1) * grp)


def _gdn_mixer_kernel(qc, kc, vc, zc, gpc, ql, kl, vl, zl, gpl, cwq, cwk, cwv, nw, yc, yl,
                      mp_s, cp_s, qp_s, op_s, dl_s, oh, sf, sb, *, Lc, Ll):
    nc, nl = Lc // GC, Ll // GC
    cwq_v, cwk_v, cwv_v = cwq[...], cwk[...], cwv[...]
    scr = (mp_s, cp_s, qp_s, op_s, dl_s)
    _gdn_precompute(qc, kc, vc, gpc, cwq_v, cwk_v, cwv_v, 0, nc, *scr)
    _gdn_precompute(ql, kl, vl, gpl, cwq_v, cwk_v, cwv_v, Lc, nl, *scr)
    sf[...] = jnp.zeros_like(sf)
    sb[...] = jnp.zeros_like(sb)
    nw_v = nw[...]
    s_refs = (sf, sb)

    def scan(z_ref, y_ref, base, n):
        def steps(cs):
            rbs = [_rows(c, base, GC) for c in cs]
            cis = [base // GC + c for c in cs]
            rms = [_rows(ci, 0, DH) for ci in cis]
            ss = [r[...] for r in s_refs]
            res = [_mm(jnp.concatenate([mp_s[d, rms[d], :], qp_s[d, rbs[d], :]], axis=0),
                       ss[d].astype(BF16)) for d in range(2)]
            outs = [res[d][DH:DH + GC] + op_s[d, rbs[d], :] for d in range(2)]
            for d in range(2):
                dl = dl_s[d, pl.ds(cis[d], 1), :]
                s_refs[d][...] = ss[d] * dl - res[d][0:DH] + cp_s[d, rms[d], :].astype(F32)
            return rbs, outs

        def first_visit(cs):
            rbs, outs = steps(cs)
            for d in range(2):
                oh[rbs[d], :] = outs[d]

        def second_visit(cs):
            rbs, outs = steps(cs)
            for d in range(2):
                oh[rbs[d], :] = outs[d] + oh[rbs[d], :]

        def finish(cs):
            for c in cs:
                o = oh[_rows(c, base, GC), :]
                y = o * lax.rsqrt(jnp.mean(o * o, axis=-1, keepdims=True) + EPS) * nw_v
                r = _rows(c, 0, GC)
                y_ref[0, r, :] = (y * _silu(z_ref[0, r, :].astype(F32))).astype(y_ref.dtype)

        def first_body(i, carry):
            first_visit((i, n - 1 - i))
            return carry

        def second_body(i, carry):
            finish((i - 1, n - i))
            second_visit((i, n - 1 - i))
            return carry

        half = n // 2
        lax.fori_loop(0, half, first_body, 0)
        second_visit((half, n - 1 - half))
        lax.fori_loop(half + 1, n, second_body, 0)
        finish((n - 1, 0))

    scan(zc, yc, 0, nc)
    scan(zl, yl, Lc, nl)


def _gdn_mixer(pc, grc, plat, grl, conv_w, norm_w):
    B, Lc, _ = pc.shape
    Ll = plat.shape[1]

    def pack_rows(gr, L):
        g = gr.reshape(B, 2, 2, HEADS, L // GC, GC).transpose(0, 3, 4, 1, 2, 5)
        g = g.reshape(B, HEADS, L // GC, 2, 2 * GC)
        return jnp.pad(g, ((0, 0), (0, 0), (0, 0), (0, 6), (0, 0)))

    def seq_specs(L, mode):
        head_blk = lambda k: pl.BlockSpec((1, L, DH), lambda b, h, k=k: (b, 0, k * HEADS + h),
                                          pipeline_mode=mode)
        return [head_blk(0), head_blk(1), head_blk(2), head_blk(3),
                pl.BlockSpec((1, 1, L // GC, 8, 2 * GC), lambda b, h: (b, h, 0, 0, 0))]

    cw_blk = lambda k: pl.BlockSpec((4, DH), lambda b, h, k=k: (0, k * HEADS + h))
    y_spec = lambda L: pl.BlockSpec((1, L, DH), lambda b, h: (b, 0, h))
    Lt = Lc + Ll
    per_dir = lambda w, dt: pltpu.VMEM((2, Lt, w), dt)
    return pl.pallas_call(
        functools.partial(_gdn_mixer_kernel, Lc=Lc, Ll=Ll),
        grid=(B, HEADS),
        in_specs=seq_specs(Lc, None) + seq_specs(Ll, pl.Buffered(1)) + [
            cw_blk(0), cw_blk(1), cw_blk(2), pl.BlockSpec((1, DH), lambda b, h: (0, 0))],
        out_specs=[y_spec(Lc), y_spec(Ll)],
        out_shape=[jax.ShapeDtypeStruct((B, Lc, HEADS * DH), BF16),
                   jax.ShapeDtypeStruct((B, Ll, HEADS * DH), BF16)],
        scratch_shapes=[pltpu.VMEM((2, Lt // GC * DH, DH), BF16),
                        pltpu.VMEM((2, Lt // GC * DH, DH), BF16),
                        per_dir(DH, BF16), per_dir(DH, F32),
                        pltpu.VMEM((2, Lt // GC, DH), F32),
                        pltpu.VMEM((Lt, DH), F32), pltpu.VMEM((DH, DH), F32),
                        pltpu.VMEM((DH, DH), F32)],
        compiler_params=_cparams("parallel", "arbitrary"),
        name="gdn_mixer",
    )(pc, pc, pc, pc, pack_rows(grc, Lc), plat, plat, plat, plat, pack_rows(grl, Ll),
      conv_w, conv_w, conv_w, norm_w)


SCAN_GROUP = 4


def _hg_chunk_parts(q16, logf, v16, rev):
    rows = lax.broadcasted_iota(jnp.int32, (CHUNK, DH), 0)
    gcum = _chunk_cumsum(logf, rows, rev)
    mid = CHUNK // 2 - 1 if rev else CHUNK // 2
    end = 0 if rev else CHUNK - 1
    ref = gcum[mid:mid + 1, :]
    glast = gcum[end:end + 1, :]
    q = q16.astype(F32)
    k = 1.0 - jnp.exp(logf)
    ii = lax.broadcasted_iota(jnp.int32, (CHUNK, CHUNK), 0)
    jj = lax.broadcasted_iota(jnp.int32, (CHUNK, CHUNK), 1)
    incl = (ii <= jj) if rev else (ii >= jj)
    scores = _mm_nt((q * jnp.exp(gcum - ref)).astype(BF16), (k * jnp.exp(ref - gcum)).astype(BF16))
    intra = _mm(jnp.where(incl, scores, 0.0).astype(BF16), v16)
    qd = (q * jnp.exp(gcum)).astype(BF16)
    upd = _mm_tn(v16, (k * jnp.exp(glast - gcum)).astype(BF16))
    return intra, qd, upd, jnp.exp(glast)


def _hg_mixer_kernel(qc, f0c, f1c, vc, zc, ql, f0l, f1l, vl, zl, nw, yc, yl, oh, sf, sb,
                     *, Lc, Ll):
    nc, nl = Lc // CHUNK, Ll // CHUNK
    sf[...] = jnp.zeros_like(sf)
    sb[...] = jnp.zeros_like(sb)
    nw_v = nw[...]

    def scan(q_ref, f_refs, v_ref, z_ref, y_ref, base, n):
        grp = min(SCAN_GROUP, n)
        n_groups = n // grp

        def advance(i):
            res = []
            for d, s_ref in enumerate((sf, sb)):
                cs = [i * grp + g for g in range(grp)] if d == 0 else \
                    [n - 1 - i * grp - g for g in range(grp)]
                parts = [_hg_chunk_parts(q_ref[0, _rows(c), :], f_refs[d][0, _rows(c), :],
                                         v_ref[0, _rows(c), :], d == 1) for c in cs]
                st = s_ref[...]
                outs = []
                for intra, qd, upd, dl in parts:
                    outs.append(intra + _mm_nt(qd, st.astype(BF16)))
                    st = st * dl + upd
                s_ref[...] = st
                res.append((cs, outs))
            return res

        def finish(c, o):
            y = o * lax.rsqrt(jnp.mean(o * o, axis=-1, keepdims=True) + EPS) * nw_v
            r = _rows(c)
            y_ref[0, r, :] = (y * _silu(z_ref[0, r, :].astype(F32))).astype(y_ref.dtype)

        def rb(c):
            return _rows(c, base)

        def first_visit(i, carry):
            for cs, outs in advance(i):
                for c, o in zip(cs, outs):
                    oh[rb(c), :] = o
            return carry

        def second_visit(i, carry):
            for cs, outs in advance(i):
                for c, o in zip(cs, outs):
                    finish(c, o + oh[rb(c), :])
            return carry

        half = n_groups // 2
        lax.fori_loop(0, half, first_visit, 0)
        if n_groups % 2:
            (cs_f, outs_f), (cs_b, outs_b) = advance(half)
            for c, o, ob in zip(cs_f, outs_f, reversed(outs_b)):
                finish(c, o + ob)
        lax.fori_loop(n_groups - half, n_groups, second_visit, 0)

    scan(qc, (f0c, f1c), vc, zc, yc, 0, nc)
    scan(ql, (f0l, f1l), vl, zl, yl, Lc, nl)


def _hg_mixer(ctx_parts, lat_parts, norm_w):
    B, Lc, _ = ctx_parts[0].shape
    Ll = lat_parts[0].shape[1]
    blk = lambda L: pl.BlockSpec((1, L, DH), lambda b, h: (b, 0, h))
    Lt = Lc + Ll
    return pl.pallas_call(
        functools.partial(_hg_mixer_kernel, Lc=Lc, Ll=Ll),
        grid=(B, HEADS),
        in_specs=[blk(Lc)] * 5 + [blk(Ll)] * 5 + [pl.BlockSpec((1, DH), lambda b, h: (0, 0))],
        out_specs=[blk(Lc), blk(Ll)],
        out_shape=[jax.ShapeDtypeStruct((B, Lc, HEADS * DH), BF16),
                   jax.ShapeDtypeStruct((B, Ll, HEADS * DH), BF16)],
        scratch_shapes=[pltpu.VMEM((Lt, DH), F32)] + [pltpu.VMEM((DH, DH), F32)] * 2,
        compiler_params=_cparams("parallel", "arbitrary"),
        name="hg_mixer",
    )(*ctx_parts, *lat_parts, norm_w)


def _lru_chunk_parts(x, w_ref, b_ref, sp, d):
    rev = d == 1
    cs = slice(2 * DH * d, 2 * DH * (d + 1))
    gates = _mm(x.astype(BF16), w_ref[0, :, cs]) + b_ref[0, :, cs]
    r = jax.nn.sigmoid(gates[:, 0:DH])
    ig = jax.nn.sigmoid(gates[:, DH:2 * DH])
    log_a = -LRU_C * r * sp
    a = jnp.exp(log_a)
    bv = jnp.sqrt(1.0 - jnp.exp(2.0 * log_a)) * (ig * x)
    rows = lax.broadcasted_iota(jnp.int32, (CHUNK, DH), 0)
    k = 1
    while k < CHUNK:
        if rev:
            m = rows < CHUNK - k
            a_sh, b_sh = _shift_rows(a, k), _shift_rows(bv, k)
        else:
            m = rows >= k
            a_sh, b_sh = _shift_rows(a, -k), _shift_rows(bv, -k)
        bv = jnp.where(m, bv + a * b_sh, bv)
        a = jnp.where(m, a * a_sh, a)
        k *= 2
    return a, bv


def _lru_mixer_kernel(xc, gc, xl, gl, cw, cb, w4, b4, lam, yc, yl, xs, of, ob, *, Lc, Ll):
    nc, nl = Lc // CHUNK, Ll // CHUNK
    cw_v = cw[...]
    cb_v = cb[...]

    def prep(x_ref, base, n):
        def fn(t):
            dst = pl.ds(base + t * CHUNK, CHUNK) if isinstance(t, int) else \
                pl.ds(pl.multiple_of(base + t * CHUNK, CHUNK), CHUNK)
            xs[dst, :] = _conv_chunk(x_ref, t, n, cw_v) + cb_v
        _for_chunks(n, fn)

    prep(xc, 0, nc)
    prep(xl, Lc, nl)
    sp0 = _softplus(-lam[0:1, :])
    sp1 = _softplus(-lam[1:2, :])

    def scan(base, n, carry):
        grp = min(SCAN_GROUP, n)

        def body(i, c):
            hs = list(c)
            for d, (sp, o_ref) in enumerate(((sp0, of), (sp1, ob))):
                cs = [i * grp + g for g in range(grp)] if d == 0 else \
                    [n - 1 - i * grp - g for g in range(grp)]
                parts = [_lru_chunk_parts(xs[_rows(t, base), :], w4, b4, sp, d) for t in cs]
                end = CHUNK - 1 if d == 0 else 0
                for t, (a_cum, b_cum) in zip(cs, parts):
                    hrows = b_cum + a_cum * hs[d]
                    o_ref[_rows(t, base), :] = hrows
                    hs[d] = hrows[end:end + 1, :]
            return tuple(hs)
        return lax.fori_loop(0, n // grp, body, carry)

    zero = jnp.zeros((1, DH), F32)
    carry = scan(0, nc, (zero, zero))
    scan(Lc, nl, carry)

    def out(g_ref, y_ref, base, n):
        def body(t, c):
            r = _rows(t)
            rb = pl.ds(pl.multiple_of(base + t * CHUNK, CHUNK), CHUNK)
            y_ref[0, r, :] = (g_ref[0, r, :].astype(F32) * (of[rb, :] + ob[rb, :])).astype(y_ref.dtype)
            return c
        lax.fori_loop(0, n, body, 0)

    out(gc, yc, 0, nc)
    out(gl, yl, Lc, nl)


def _lru_mixer(xr_c, gate_c, xr_l, gate_l, conv_w, conv_b, w4, b4, lam):
    B, Lc, _ = xr_c.shape
    Ll = xr_l.shape[1]
    blk = lambda L: pl.BlockSpec((1, L, DH), lambda b, g: (b, 0, g))
    Lt = Lc + Ll
    return pl.pallas_call(
        functools.partial(_lru_mixer_kernel, Lc=Lc, Ll=Ll),
        grid=(B, HEADS),
        in_specs=[blk(Lc), blk(Lc), blk(Ll), blk(Ll),
                  pl.BlockSpec((4, DH), lambda b, g: (0, g)),
                  pl.BlockSpec((1, DH), lambda b, g: (0, g)),
                  pl.BlockSpec((1, DH, 4 * DH), lambda b, g: (g, 0, 0)),
                  pl.BlockSpec((1, 1, 4 * DH), lambda b, g: (g, 0, 0)),
                  pl.BlockSpec((2, DH), lambda b, g: (0, g))],
        out_specs=[blk(Lc), blk(Ll)],
        out_shape=[jax.ShapeDtypeStruct((B, Lc, HEADS * DH), BF16),
                   jax.ShapeDtypeStruct((B, Ll, HEADS * DH), BF16)],
        scratch_shapes=[pltpu.VMEM((Lt, DH), F32)] * 3,
        compiler_params=_cparams("parallel", "arbitrary"),
        name="lru_mixer",
    )(xr_c, gate_c, xr_l, gate_l, conv_w, conv_b, w4, b4, lam)


def _to_scan_order(h):
    bsz, L, d = h.shape
    return h.reshape(bsz, L // GRID_W, GRID_W, d).transpose(0, 2, 1, 3).reshape(bsz, L, d)


def _from_scan_order(h):
    bsz, L, d = h.shape
    return h.reshape(bsz, GRID_W, L // GRID_W, d).transpose(0, 2, 1, 3).reshape(bsz, L, d)


def _pad_lanes(v, layout):
    row = jnp.zeros((DH,), F32)
    for off in layout:
        row = row.at[off:off + v.shape[0]].set(v)
    return row.reshape(1, DH)


def kernel(x, c, ctx, c_ctx, ada_w, ada_b, norm_mix, norm_ffn, norm_final, ffn_w1, ffn_w3, ffn_w2, gdn_w_in, gdn_conv, gdn_a_log, gdn_dt_bias, gdn_norm, gdn_w_out, lru_w_in, lru_conv_w, lru_conv_b, lru_w_r, lru_b_r, lru_w_i, lru_b_i, lru_lambda, lru_w_out, hg_w_in, hg_lb_logits, hg_norm, hg_w_out):
    B, L, d = x.shape
    depth = ada_w.shape[0]
    assert d == D_MODEL and L % GRID_W == 0 and L % GDN_CHUNK == 0 and ctx.shape[1] % GDN_CHUNK == 0
    assert B + 1 <= 8

    cvec = jnp.zeros((8, d), F32).at[0:B].set(c).at[B].set(c_ctx)
    mods = _mods(cvec, ada_w, ada_b).reshape(depth, 8, 6, d)

    xl, xc = x, ctx
    for i in range(depth):
        mod_l = mods[i, 0:B]
        mod_c = jnp.broadcast_to(mods[i, B][None], (B, 6, d))
        col = i % 2 == 1
        if col:
            xl = _to_scan_order(xl)
        nw = norm_mix[i].reshape(1, d)
        kind, j = i % 3, i // 3
        if kind == 0:
            w_in = gdn_w_in[j]
            n_main = 4 * HEADS * DH
            w_main = w_in[:, :n_main].astype(BF16)
            wa = w_in[:, n_main:n_main + 2 * HEADS]
            wb = w_in[:, n_main + 2 * HEADS:n_main + 4 * HEADS]
            wab = jnp.zeros((d, DH), F32)
            wab = wab.at[:, 0:2 * HEADS].set(wa).at[:, 2 * HEADS:4 * HEADS].set(wb)
            wab_hi = wab.astype(BF16)
            wab_lo = (wab - wab_hi.astype(F32)).astype(BF16)
            alog = _pad_lanes(gdn_a_log[j].reshape(-1), (0,))
            dtb = _pad_lanes(gdn_dt_bias[j].reshape(-1), (0,))
            pc, grc = _gdn_proj(xc, nw, mod_c, w_main, wab_hi, wab_lo, alog, dtb)
            plat, grl = _gdn_proj(xl, nw, mod_l, w_main, wab_hi, wab_lo, alog, dtb)
            yc, yl = _gdn_mixer(pc, grc, plat, grl, gdn_conv[j], gdn_norm[j].reshape(1, DH))
            w_out = gdn_w_out[j]
        elif kind == 1:
            w_in = lru_w_in[j].astype(BF16)
            gate_c, xr_c = _lru_proj(xc, nw, mod_c, w_in)
            gate_l, xr_l = _lru_proj(xl, nw, mod_l, w_in)
            w4 = jnp.concatenate([lru_w_r[j, 0], lru_w_i[j, 0], lru_w_r[j, 1], lru_w_i[j, 1]],
                                 axis=-1).astype(BF16)
            b4 = jnp.stack([lru_b_r[j, 0], lru_b_i[j, 0], lru_b_r[j, 1], lru_b_i[j, 1]], axis=0)
            b4 = b4.reshape(4, HEADS, DH).transpose(1, 0, 2).reshape(HEADS, 1, 4 * DH)
            yc, yl = _lru_mixer(xr_c, gate_c, xr_l, gate_l, lru_conv_w[j],
                                lru_conv_b[j].reshape(1, d), w4, b4, lru_lambda[j])
            w_out = lru_w_out[j]
        else:
            w_in = hg_w_in[j].astype(BF16)
            parts_c = _hg_proj(xc, nw, mod_c, w_in, hg_lb_logits, i)
            parts_l = _hg_proj(xl, nw, mod_l, w_in, hg_lb_logits, i)
            yc, yl = _hg_mixer(parts_c, parts_l, hg_norm[j].reshape(1, DH))
            w_out = hg_w_out[j]

        nwf = norm_ffn[i].reshape(1, d)
        wo, w1, w3, w2 = (w.astype(BF16) for w in (w_out, ffn_w1[i], ffn_w3[i], ffn_w2[i]))
        last = i == depth - 1
        fnw = norm_final.reshape(1, d)
        xl = _post(xl, yl, mod_l, nwf, wo, w1, w3, w2, fnw, last)
        if not last:
            xc = _post(xc, yc, mod_c, nwf, wo, w1, w3, w2, fnw, False)
        if col:
            xl = _from_scan_order(xl)
    return xl
```

```python
import functools
import math

import jax
import jax.numpy as jnp
from jax import lax
from jax.experimental import pallas as pl
from jax.experimental.pallas import tpu as pltpu

F32 = jnp.float32
BF16 = jnp.bfloat16

D_MODEL = 1024
HEADS = 8
DH = 128
CHUNK = 64
GDN_CHUNK = 128
GRID_W = 64
EPS = 1e-6
LRU_C = 8.0
FFN_HIDDEN = 2816
FFN_COLS = 256
TOKEN_TILE = 512
V7X_VMEM_BYTES = 64 * 1024 * 1024
VMEM_LIMIT = V7X_VMEM_BYTES - 6 * 1024 * 1024


def _cparams(*sem):
    return pltpu.CompilerParams(dimension_semantics=sem, vmem_limit_bytes=VMEM_LIMIT)


def _mm(a, b):
    return jnp.dot(a, b, preferred_element_type=F32)


def _mm_nt(a, b):
    return lax.dot_general(a, b, (((1,), (1,)), ((), ())), preferred_element_type=F32)


def _mm_tn(a, b):
    return lax.dot_general(a, b, (((0,), (0,)), ((), ())), preferred_element_type=F32)


def _silu(x):
    return x * jax.nn.sigmoid(x)


def _softplus(x):
    return jnp.maximum(x, 0.0) + jnp.log(1.0 + jnp.exp(-jnp.abs(x)))


def _gelu_tanh(x):
    return 0.5 * x * (1.0 + jnp.tanh(math.sqrt(2.0 / math.pi) * (x + 0.044715 * (x * x * x))))


def _norm_mod(x, nw, shift, scale):
    ms = jnp.mean(x * x, axis=-1, keepdims=True)
    y = x * lax.rsqrt(ms + EPS) * nw
    return y * (1.0 + scale) + shift


def _shift_rows(x, k):
    n = x.shape[0]
    return x if k % n == 0 else pltpu.roll(x, (-k) % n, 0)


def _chunk_cumsum(x, rows_in_chunk, rev, chunk=CHUNK):
    k = 1
    while k < chunk:
        if rev:
            x = x + jnp.where(rows_in_chunk < chunk - k, _shift_rows(x, k), 0.0)
        else:
            x = x + jnp.where(rows_in_chunk >= k, _shift_rows(x, -k), 0.0)
        k *= 2
    return x


def _conv_chunk(ref, t, n_chunks, cw, chunk=CHUNK):
    first = isinstance(t, int) and t == 0
    last = isinstance(t, int) and t == n_chunks - 1
    halo = 16
    if first:
        win = ref[0, 0:chunk + halo, :].astype(F32)
        off = 0
    elif last:
        start = t * chunk - halo
        win = ref[0, start:start + chunk + halo, :].astype(F32)
        off = halo
    else:
        start = t * chunk - halo
        if not isinstance(t, int):
            start = pl.multiple_of(start, halo)
        win = ref[0, pl.ds(start, chunk + 2 * halo), :].astype(F32)
        off = halo
    rows = lax.broadcasted_iota(jnp.int32, (chunk, win.shape[1]), 0)

    def tap(k):
        return _shift_rows(win, k)[off:off + chunk]

    xm2, xm1, x0, xp1 = tap(-2), tap(-1), tap(0), tap(1)
    if first:
        xm2 = jnp.where(rows >= 2, xm2, 0.0)
        xm1 = jnp.where(rows >= 1, xm1, 0.0)
    if last:
        xp1 = jnp.where(rows < chunk - 1, xp1, 0.0)
    return cw[0:1] * xm2 + cw[1:2] * xm1 + cw[2:3] * x0 + cw[3:4] * xp1


def _for_chunks(n_chunks, fn):
    fn(0)
    if n_chunks > 2:
        def body(t, carry):
            fn(t)
            return carry
        lax.fori_loop(1, n_chunks - 1, body, 0)
    if n_chunks > 1:
        fn(n_chunks - 1)


def _rows(t, base=0, chunk=CHUNK):
    start = base + t * chunk
    if not isinstance(start, int):
        start = pl.multiple_of(start, chunk)
    return pl.ds(start, chunk)


def _gated_norm_out(of, ob, z_ref, y_ref, nw, base, n_chunks):
    def body(t, carry):
        r = _rows(t)
        rb = pl.ds(pl.multiple_of(base + t * CHUNK, CHUNK), CHUNK)
        o = of[rb, :] + ob[rb, :]
        ms = jnp.mean(o * o, axis=-1, keepdims=True)
        y = o * lax.rsqrt(ms + EPS) * nw
        y_ref[0, r, :] = (y * _silu(z_ref[0, r, :].astype(F32))).astype(y_ref.dtype)
        return carry
    lax.fori_loop(0, n_chunks, body, 0)


def _mods_kernel(cv_ref, w_ref, b_ref, o_ref):
    s = _silu(cv_ref[...])
    o_ref[0] = jnp.dot(s, w_ref[0], preferred_element_type=F32,
                       precision=lax.Precision.HIGHEST) + b_ref[0]


def _mods(cvec, ada_w, ada_b):
    depth, d, n = ada_w.shape
    tn = 1536
    return pl.pallas_call(
        _mods_kernel,
        grid=(depth, n // tn),
        in_specs=[
            pl.BlockSpec((8, d), lambda i, j: (0, 0)),
            pl.BlockSpec((1, d, tn), lambda i, j: (i, 0, j)),
            pl.BlockSpec((1, 1, tn), lambda i, j: (i, 0, j)),
        ],
        out_specs=pl.BlockSpec((1, 8, tn), lambda i, j: (i, 0, j)),
        out_shape=jax.ShapeDtypeStruct((depth, 8, n), F32),
        compiler_params=_cparams("parallel", "parallel"),
        name="adaln_mods",
    )(cvec, ada_w, ada_b.reshape(depth, 1, n))


def _token_tile(L):
    return min(TOKEN_TILE, L)


def _proj_in_specs(tm, d):
    return [
        pl.BlockSpec((1, tm, d), lambda b, t: (b, t, 0)),
        pl.BlockSpec((1, d), lambda b, t: (0, 0)),
        pl.BlockSpec((1, 6, d), lambda b, t: (b, 0, 0)),
    ]


def _full(shape):
    nd = len(shape)
    return pl.BlockSpec(shape, lambda b, t: (0,) * nd)


HALO = 8


def _gdn_proj_kernel(x_ref, xp_ref, xn_ref, nw_ref, mod_ref, w_ref, cw_ref, wabh_ref, wabl_ref,
                     alog_ref, dtb_ref, qkvz_ref, grow_ref):
    t, nt = pl.program_id(1), pl.num_programs(1)
    tm = x_ref.shape[1]
    nw, shift, scale = nw_ref[...], mod_ref[0, 0:1], mod_ref[0, 1:2]
    h_prev = _norm_mod(xp_ref[0], nw, shift, scale) * (t > 0).astype(F32)
    h_next = _norm_mod(xn_ref[0], nw, shift, scale) * (t < nt - 1).astype(F32)
    ha = jnp.concatenate([h_prev, _norm_mod(x_ref[0], nw, shift, scale), h_next], axis=0)
    hab = ha.astype(BF16)
    for c in range(3):
        cs = slice(c * D_MODEL, (c + 1) * D_MODEL)
        p = _mm(hab, w_ref[:, cs])
        cw = cw_ref[:, cs]
        y = cw[0:1] * _shift_rows(p, -2)[HALO:HALO + tm]
        y = y + cw[1:2] * _shift_rows(p, -1)[HALO:HALO + tm]
        y = y + cw[2:3] * p[HALO:HALO + tm]
        y = _silu(y + cw[3:4] * _shift_rows(p, 1)[HALO:HALO + tm])
        for hd in range(HEADS):
            hs = slice(hd * DH, (hd + 1) * DH)
            seg = y[:, hs]
            if c == 0:
                seg = seg * lax.rsqrt(jnp.sum(seg * seg, axis=-1, keepdims=True) + EPS) * (DH ** -0.5)
            elif c == 1:
                seg = seg * lax.rsqrt(jnp.sum(seg * seg, axis=-1, keepdims=True) + EPS)
            qkvz_ref[0, :, c * D_MODEL + hd * DH:c * D_MODEL + (hd + 1) * DH] = \
                seg.astype(qkvz_ref.dtype)
    h = ha[HALO:HALO + tm]
    hb = h.astype(BF16)
    qkvz_ref[0, :, 3 * D_MODEL:4 * D_MODEL] = \
        _mm(hb, w_ref[:, 3 * D_MODEL:4 * D_MODEL]).astype(qkvz_ref.dtype)
    hl = (h - hb.astype(F32)).astype(BF16)
    ab = _mm(hb, wabh_ref[...]) + _mm(hl, wabh_ref[...]) + _mm(hb, wabl_ref[...])
    lane = lax.broadcasted_iota(jnp.int32, ab.shape, 1)
    rin = lax.broadcasted_iota(jnp.int32, ab.shape, 0) & (GDN_CHUNK - 1)
    g = -jnp.exp(alog_ref[...]) * _softplus(ab + dtb_ref[...])
    res = jnp.where(lane < HEADS, _chunk_cumsum(g, rin, False, GDN_CHUNK),
                    jnp.where(lane < 2 * HEADS, _chunk_cumsum(g, rin, True, GDN_CHUNK),
                              jax.nn.sigmoid(ab)))
    grow_ref[0] = res.T[0:4 * HEADS, :]


def _gdn_proj(x, nw, mods, w_main, conv_w, wab_hi, wab_lo, alog, dtb):
    B, L, d = x.shape
    tm = _token_tile(L)
    n = w_main.shape[1]
    per_tile = tm // HALO
    last_blk = L // HALO - 1
    x_spec, nw_spec, mod_spec = _proj_in_specs(tm, d)
    prev_spec = pl.BlockSpec((1, HALO, d), lambda b, t: (b, jnp.maximum(t * per_tile - 1, 0), 0))
    next_spec = pl.BlockSpec((1, HALO, d),
                             lambda b, t: (b, jnp.minimum((t + 1) * per_tile, last_blk), 0))
    return pl.pallas_call(
        _gdn_proj_kernel,
        grid=(B, L // tm),
        in_specs=[x_spec, prev_spec, next_spec, nw_spec, mod_spec,
                  pl.BlockSpec(w_main.shape, lambda b, t: (0, 0), pipeline_mode=pl.Buffered(1)),
                  _full(conv_w.shape), _full(wab_hi.shape), _full(wab_lo.shape),
                  _full(alog.shape), _full(dtb.shape)],
        out_specs=[
            pl.BlockSpec((1, tm, n), lambda b, t: (b, t, 0)),
            pl.BlockSpec((1, 4 * HEADS, tm), lambda b, t: (b, 0, t)),
        ],
        out_shape=[
            jax.ShapeDtypeStruct((B, L, n), BF16),
            jax.ShapeDtypeStruct((B, 4 * HEADS, L), F32),
        ],
        compiler_params=_cparams("parallel", "parallel"),
        name="gdn_proj",
    )(x, x, x, nw, mods, w_main, conv_w, wab_hi, wab_lo, alog, dtb)


def _lru_proj_kernel(x_ref, nw_ref, mod_ref, w_ref, gate_ref, xr_ref):
    h = _norm_mod(x_ref[0], nw_ref[...], mod_ref[0, 0:1], mod_ref[0, 1:2])
    hb = h.astype(BF16)
    gate_ref[0] = _gelu_tanh(_mm(hb, w_ref[:, 0:D_MODEL])).astype(gate_ref.dtype)
    xr_ref[0] = _mm(hb, w_ref[:, D_MODEL:2 * D_MODEL])


def _lru_proj(x, nw, mods, w):
    B, L, d = x.shape
    tm = _token_tile(L)
    spec = pl.BlockSpec((1, tm, d), lambda b, t: (b, t, 0))
    return pl.pallas_call(
        _lru_proj_kernel,
        grid=(B, L // tm),
        in_specs=_proj_in_specs(tm, d) + [_full(w.shape)],
        out_specs=[spec, spec],
        out_shape=[jax.ShapeDtypeStruct((B, L, d), BF16), jax.ShapeDtypeStruct((B, L, d), F32)],
        compiler_params=_cparams("parallel", "parallel"),
        name="lru_proj",
    )(x, nw, mods, w)


def _hg_proj_kernel(x_ref, nw_ref, mod_ref, w_ref, lbl_ref, q_ref, f0_ref, f1_ref, v_ref, z_ref,
                    *, layer):
    logits = lbl_ref[...]
    depth = logits.shape[0]
    m = logits[0:1]
    for r in range(1, depth):
        m = jnp.maximum(m, logits[r:r + 1])
    e = jnp.exp(logits - m)
    tot = e[0:1]
    for r in range(1, depth):
        tot = tot + e[r:r + 1]
    lb = jnp.zeros_like(tot)
    for r in range(1, layer + 1):
        lb = lb + e[r:r + 1] / tot

    h = _norm_mod(x_ref[0], nw_ref[...], mod_ref[0, 0:1], mod_ref[0, 1:2])
    hb = h.astype(BF16)

    def cols(c):
        return _mm(hb, w_ref[:, c * D_MODEL:(c + 1) * D_MODEL])

    q_ref[0] = _silu(cols(0)).astype(q_ref.dtype)
    for d, f_ref in enumerate((f0_ref, f1_ref)):
        f = lb + (1.0 - lb) * jax.nn.sigmoid(cols(1 + d))
        f_ref[0] = jnp.log(f)
    v_ref[0] = cols(3).astype(v_ref.dtype)
    z_ref[0] = cols(4).astype(z_ref.dtype)


def _hg_proj(x, nw, mods, w, lb_logits, layer):
    B, L, d = x.shape
    tm = _token_tile(L)
    spec = pl.BlockSpec((1, tm, d), lambda b, t: (b, t, 0))
    sds = lambda dt: jax.ShapeDtypeStruct((B, L, d), dt)
    return pl.pallas_call(
        functools.partial(_hg_proj_kernel, layer=layer),
        grid=(B, L // tm),
        in_specs=_proj_in_specs(tm, d) + [_full(w.shape), _full(lb_logits.shape)],
        out_specs=[spec] * 5,
        out_shape=[sds(BF16), sds(F32), sds(F32), sds(BF16), sds(BF16)],
        compiler_params=_cparams("parallel", "parallel"),
        name="hg_proj",
    )(x, nw, mods, w, lb_logits)


def _post_kernel(x_ref, y_ref, mod_ref, nw_ref, wo_ref, w1_ref, w3_ref, w2_ref, fnw_ref,
                 o_ref, acc_ref, *, final):
    x1 = x_ref[0] + mod_ref[0, 2:3] * _mm(y_ref[0], wo_ref[...])
    hb = _norm_mod(x1, nw_ref[...], mod_ref[0, 3:4], mod_ref[0, 4:5]).astype(BF16)
    hidden = w1_ref.shape[1]
    for j in range(hidden // FFN_COLS):
        cs = slice(j * FFN_COLS, (j + 1) * FFN_COLS)
        a = (_silu(_mm(hb, w1_ref[:, cs])) * _mm(hb, w3_ref[:, cs])).astype(BF16)
        part = _mm(a, w2_ref[cs, :])
        if j == 0:
            acc_ref[...] = part
        else:
            acc_ref[...] += part
    out = x1 + mod_ref[0, 5:6] * acc_ref[...]
    if final:
        out = out * lax.rsqrt(jnp.mean(out * out, axis=-1, keepdims=True) + EPS) * fnw_ref[...]
    o_ref[0] = out


def _post(x, y, mods, nw, wo, w1, w3, w2, final_nw, final):
    B, L, d = x.shape
    tm = _token_tile(L)
    tile = pl.BlockSpec((1, tm, d), lambda b, t: (b, t, 0))

    def resident(shape):
        return pl.BlockSpec(shape, lambda b, t: (0, 0), pipeline_mode=pl.Buffered(1))

    return pl.pallas_call(
        functools.partial(_post_kernel, final=final),
        grid=(B, L // tm),
        in_specs=[tile, tile,
                  pl.BlockSpec((1, 6, d), lambda b, t: (b, 0, 0)),
                  pl.BlockSpec((1, d), lambda b, t: (0, 0)),
                  resident(wo.shape), resident(w1.shape), resident(w3.shape), resident(w2.shape),
                  pl.BlockSpec((1, d), lambda b, t: (0, 0))],
        out_specs=tile,
        out_shape=jax.ShapeDtypeStruct((B, L, d), F32),
        scratch_shapes=[pltpu.VMEM((tm, d), F32)],
        compiler_params=_cparams("parallel", "parallel"),
        name="post_ffn",
    )(x, y, mods, nw, wo, w1, w3, w2, final_nw)


GC = GDN_CHUNK
GDN_GROUP = 8


def _block_diag(m):
    left = lax.broadcasted_iota(jnp.int32, m.shape, 1) < GC
    return jnp.concatenate([jnp.where(left, m, 0.0), jnp.where(left, 0.0, m)], axis=0).astype(BF16)


def _dup_diag(m16):
    z = jnp.zeros_like(m16)
    return jnp.concatenate([jnp.concatenate([m16, z], axis=1),
                            jnp.concatenate([z, m16], axis=1)], axis=0)


def _packed_masks():
    ii = lax.broadcasted_iota(jnp.int32, (GC, 2 * GC), 0)
    jj = lax.broadcasted_iota(jnp.int32, (GC, 2 * GC), 1)
    jl = jj & (GC - 1)
    bwd = jj >= GC
    fwd = jnp.logical_not(bwd)
    incl = (fwd & (ii >= jl)) | (bwd & (ii <= jl))
    strict = (fwd & (ii > jl)) | (bwd & (ii < jl))
    same = {b: (ii & -b) == (jl & -b) for b in (4, 8, 16, 32, 64, 128) if b <= GC}
    return dict(left=fwd, incl=incl, strict=strict, eye=ii == jl, same=same)


def _tri_inverse_packed(nms, masks):
    same = masks["same"]
    n4 = [jnp.where(same[4], n, 0.0) for n in nms]
    n4d = [_block_diag(a) for a in n4]
    p2 = [_mm(a.astype(BF16), d) for a, d in zip(n4, n4d)]
    p3 = [_mm(p.astype(BF16), d) for p, d in zip(p2, n4d)]
    eye = jnp.where(masks["eye"], 1.0, 0.0)
    xs = [eye + a + b + c for a, b, c in zip(n4, p2, p3)]
    b = 4
    while b < GC:
        off_block = same[2 * b] & jnp.logical_not(same[b])
        cms = [jnp.where(off_block, n, 0.0).astype(BF16) for n in nms]
        xds = [_block_diag(x) for x in xs]
        yds = [_block_diag(_mm(c, d)) for c, d in zip(cms, xds)]
        xs = [x + _mm(x.astype(BF16), d) for x, d in zip(xs, yds)]
        b *= 2
    return xs


def _gdn_precompute(q_ref, k_ref, v_ref, gp_ref, base, n_chunks,
                    mp_s, cp_s, qp_s, op_s, dl_s):
    masks = _packed_masks()
    left, incl, strict = masks["left"], masks["incl"], masks["strict"]
    grp = min(GDN_GROUP, n_chunks)

    def group(t0):
        ts = [t0 + g for g in range(grp)]
        tiles = [gp_ref[0, 0, t] for t in ts]
        cols = [tl.T for tl in tiles]
        k16s = [k_ref[0, _rows(t, 0, GC), :] for t in ts]
        q16s = [q_ref[0, _rows(t, 0, GC), :] for t in ts]
        v16s = [v_ref[0, _rows(t, 0, GC), :] for t in ts]
        ks = [k.astype(F32) for k in k16s]
        qs = [q.astype(F32) for q in q16s]
        kqs = [_mm_nt(jnp.concatenate([k, q], axis=0), jnp.concatenate([k, k], axis=0))
               for k, q in zip(k16s, q16s)]
        g_rows = [tl[0:1, :] for tl in tiles]
        b_rows = [tl[1:2, :] for tl in tiles]
        g_cols = [jnp.where(left, c[0:GC, 0:1], c[GC:2 * GC, 0:1]) for c in cols]
        b_cols = [jnp.where(left, c[0:GC, 1:2], c[GC:2 * GC, 1:2]) for c in cols]
        decays = [jnp.where(incl, jnp.exp(jnp.where(incl, gc - gr, 0.0)), 0.0)
                  for gc, gr in zip(g_cols, g_rows)]
        nms = [jnp.where(strict, -(kq[0:GC] * bc * dc), 0.0)
               for kq, bc, dc in zip(kqs, b_cols, decays)]
        tinvs = _tri_inverse_packed(nms, masks)
        tus = [(x * br).astype(BF16) for x, br in zip(tinvs, b_rows)]
        tws = [(x * (br * jnp.exp(gr))).astype(BF16) for x, br, gr in zip(tinvs, b_rows, g_rows)]
        uus = [_mm(tu, _dup_diag(v)) for tu, v in zip(tus, v16s)]
        wws = [_mm(tw, _dup_diag(k)) for tw, k in zip(tws, k16s)]
        u16s = [u.astype(BF16) for u in uus]
        w16s = [w.astype(BF16) for w in wws]
        a16s = [jnp.where(incl, kq[GC:2 * GC] * dc, 0.0).astype(BF16)
                for kq, dc in zip(kqs, decays)]
        zero = jnp.zeros((GC, DH), BF16)
        aus = [_mm(a, jnp.concatenate(
            [jnp.concatenate([w[:, 0:DH], u[:, 0:DH], zero, zero], axis=1),
             jnp.concatenate([zero, zero, w[:, DH:2 * DH], u[:, DH:2 * DH]], axis=1)], axis=0))
            for a, w, u in zip(a16s, w16s, u16s)]
        g_tots = [(tl[0:1, GC - 1:GC], tl[0:1, GC:GC + 1]) for tl in tiles]
        g_colds = [(c[0:GC, 0:1], c[GC:2 * GC, 0:1]) for c in cols]
        kes = [[(k * jnp.exp(gt[d] - gc[d])).astype(BF16) for d in range(2)]
               for k, gt, gc in zip(ks, g_tots, g_colds)]
        kwus = [[_mm_tn(ke[d], jnp.concatenate([w[:, d * DH:(d + 1) * DH],
                                                u[:, d * DH:(d + 1) * DH]], axis=1))
                 for d in range(2)] for ke, w, u in zip(kes, w16s, u16s)]
        for g, t in enumerate(ts):
            dst = _rows(t, base, GC)
            ci = base // GC + t
            dst2 = pl.ds(ci * DH, DH) if isinstance(ci, int) else \
                pl.ds(pl.multiple_of(ci * DH, DH), DH)
            for d in range(2):
                qe = qs[g] * jnp.exp(g_colds[g][d])
                qp_s[d, dst, :] = (qe - aus[g][:, 2 * d * DH:(2 * d + 1) * DH]).astype(BF16)
                op_s[d, dst, :] = aus[g][:, (2 * d + 1) * DH:(2 * d + 2) * DH]
                mp_s[d, dst2, :] = kwus[g][d][:, 0:DH].astype(BF16)
                cp_s[d, dst2, :] = kwus[g][d][:, DH:2 * DH].astype(BF16)
                dl_s[d, pl.ds(ci, 1), :] = jnp.broadcast_to(jnp.exp(g_tots[g][d]), (1, DH))

    n_groups = n_chunks // grp
    group(0)
    if n_groups > 2:
        def body(i, carry):
            group(i * grp)
            return carry
        lax.fori_loop(1, n_groups - 1, body, 0)
    if n_groups > 1:
        group((n_groups - 1) * grp)


def _gdn_mixer_kernel(qc, kc, vc, zc, gpc, ql, kl, vl, zl, gpl, nw, yc, yl,
                      mp_s, cp_s, qp_s, op_s, dl_s, oh, sf, sb, *, Lc, Ll):
    nc, nl = Lc // GC, Ll // GC
    scr = (mp_s, cp_s, qp_s, op_s, dl_s)
    _gdn_precompute(qc, kc, vc, gpc, 0, nc, *scr)
    _gdn_precompute(ql, kl, vl, gpl, Lc, nl, *scr)
    sf[...] = jnp.zeros_like(sf)
    sb[...] = jnp.zeros_like(sb)
    nw_v = nw[...]
    s_refs = (sf, sb)

    def scan(z_ref, y_ref, base, n):
        def steps(cs):
            rbs = [_rows(c, base, GC) for c in cs]
            cis = [base // GC + c for c in cs]
            rms = [_rows(ci, 0, DH) for ci in cis]
            ss = [r[...] for r in s_refs]
            res = [_mm(jnp.concatenate([mp_s[d, rms[d], :], qp_s[d, rbs[d], :]], axis=0),
                       ss[d].astype(BF16)) for d in range(2)]
            outs = [res[d][DH:DH + GC] + op_s[d, rbs[d], :] for d in range(2)]
            for d in range(2):
                dl = dl_s[d, pl.ds(cis[d], 1), :]
                s_refs[d][...] = ss[d] * dl - res[d][0:DH] + cp_s[d, rms[d], :].astype(F32)
            return rbs, outs

        def first_visit(cs):
            rbs, outs = steps(cs)
            for d in range(2):
                oh[rbs[d], :] = outs[d]

        def second_visit(cs):
            rbs, outs = steps(cs)
            for d in range(2):
                oh[rbs[d], :] = outs[d] + oh[rbs[d], :]

        def finish(cs):
            for c in cs:
                o = oh[_rows(c, base, GC), :]
                y = o * lax.rsqrt(jnp.mean(o * o, axis=-1, keepdims=True) + EPS) * nw_v
                r = _rows(c, 0, GC)
                y_ref[0, r, :] = (y * _silu(z_ref[0, r, :].astype(F32))).astype(y_ref.dtype)

        def first_body(i, carry):
            first_visit((i, n - 1 - i))
            return carry

        def second_body(i, carry):
            finish((i - 1, n - i))
            second_visit((i, n - 1 - i))
            return carry

        half = n // 2
        lax.fori_loop(0, half, first_body, 0)
        second_visit((half, n - 1 - half))
        lax.fori_loop(half + 1, n, second_body, 0)
        finish((n - 1, 0))

    scan(zc, yc, 0, nc)
    scan(zl, yl, Lc, nl)


def _gdn_mixer(pc, grc, plat, grl, norm_w):
    B, Lc, _ = pc.shape
    Ll = plat.shape[1]

    def pack_rows(gr, L):
        g = gr.reshape(B, 2, 2, HEADS, L // GC, GC).transpose(0, 3, 4, 1, 2, 5)
        g = g.reshape(B, HEADS, L // GC, 2, 2 * GC)
        return jnp.pad(g, ((0, 0), (0, 0), (0, 0), (0, 6), (0, 0)))

    def seq_specs(L, mode):
        head_blk = lambda k: pl.BlockSpec((1, L, DH), lambda b, h, k=k: (b, 0, k * HEADS + h),
                                          pipeline_mode=mode)
        return [head_blk(0), head_blk(1), head_blk(2), head_blk(3),
                pl.BlockSpec((1, 1, L // GC, 8, 2 * GC), lambda b, h: (b, h, 0, 0, 0))]

    y_spec = lambda L: pl.BlockSpec((1, L, DH), lambda b, h: (b, 0, h))
    Lt = Lc + Ll
    per_dir = lambda w, dt: pltpu.VMEM((2, Lt, w), dt)
    return pl.pallas_call(
        functools.partial(_gdn_mixer_kernel, Lc=Lc, Ll=Ll),
        grid=(B, HEADS),
        in_specs=seq_specs(Lc, None) + seq_specs(Ll, pl.Buffered(1)) + [
            pl.BlockSpec((1, DH), lambda b, h: (0, 0))],
        out_specs=[y_spec(Lc), y_spec(Ll)],
        out_shape=[jax.ShapeDtypeStruct((B, Lc, HEADS * DH), BF16),
                   jax.ShapeDtypeStruct((B, Ll, HEADS * DH), BF16)],
        scratch_shapes=[pltpu.VMEM((2, Lt // GC * DH, DH), BF16),
                        pltpu.VMEM((2, Lt // GC * DH, DH), BF16),
                        per_dir(DH, BF16), per_dir(DH, F32),
                        pltpu.VMEM((2, Lt // GC, DH), F32),
                        pltpu.VMEM((Lt, DH), F32), pltpu.VMEM((DH, DH), F32),
                        pltpu.VMEM((DH, DH), F32)],
        compiler_params=_cparams("parallel", "arbitrary"),
        name="gdn_mixer",
    )(pc, pc, pc, pc, pack_rows(grc, Lc), plat, plat, plat, plat, pack_rows(grl, Ll),
      norm_w)


SCAN_GROUP = 4


def _hg_chunk_parts(q16, logf, v16, rev):
    rows = lax.broadcasted_iota(jnp.int32, (CHUNK, DH), 0)
    gcum = _chunk_cumsum(logf, rows, rev)
    mid = CHUNK // 2 - 1 if rev else CHUNK // 2
    end = 0 if rev else CHUNK - 1
    ref = gcum[mid:mid + 1, :]
    glast = gcum[end:end + 1, :]
    q = q16.astype(F32)
    k = 1.0 - jnp.exp(logf)
    ii = lax.broadcasted_iota(jnp.int32, (CHUNK, CHUNK), 0)
    jj = lax.broadcasted_iota(jnp.int32, (CHUNK, CHUNK), 1)
    incl = (ii <= jj) if rev else (ii >= jj)
    scores = _mm_nt((q * jnp.exp(gcum - ref)).astype(BF16), (k * jnp.exp(ref - gcum)).astype(BF16))
    intra = _mm(jnp.where(incl, scores, 0.0).astype(BF16), v16)
    qd = (q * jnp.exp(gcum)).astype(BF16)
    upd = _mm_tn(v16, (k * jnp.exp(glast - gcum)).astype(BF16))
    return intra, qd, upd, jnp.exp(glast)


def _hg_mixer_kernel(qc, f0c, f1c, vc, zc, ql, f0l, f1l, vl, zl, nw, yc, yl, oh, sf, sb,
                     *, Lc, Ll):
    nc, nl = Lc // CHUNK, Ll // CHUNK
    sf[...] = jnp.zeros_like(sf)
    sb[...] = jnp.zeros_like(sb)
    nw_v = nw[...]

    def scan(q_ref, f_refs, v_ref, z_ref, y_ref, base, n):
        grp = min(SCAN_GROUP, n)
        n_groups = n // grp

        def advance(i):
            res = []
            for d, s_ref in enumerate((sf, sb)):
                cs = [i * grp + g for g in range(grp)] if d == 0 else \
                    [n - 1 - i * grp - g for g in range(grp)]
                parts = [_hg_chunk_parts(q_ref[0, _rows(c), :], f_refs[d][0, _rows(c), :],
                                         v_ref[0, _rows(c), :], d == 1) for c in cs]
                st = s_ref[...]
                outs = []
                for intra, qd, upd, dl in parts:
                    outs.append(intra + _mm_nt(qd, st.astype(BF16)))
                    st = st * dl + upd
                s_ref[...] = st
                res.append((cs, outs))
            return res

        def finish(c, o):
            y = o * lax.rsqrt(jnp.mean(o * o, axis=-1, keepdims=True) + EPS) * nw_v
            r = _rows(c)
            y_ref[0, r, :] = (y * _silu(z_ref[0, r, :].astype(F32))).astype(y_ref.dtype)

        def rb(c):
            return _rows(c, base)

        def first_visit(i, carry):
            for cs, outs in advance(i):
                for c, o in zip(cs, outs):
                    oh[rb(c), :] = o
            return carry

        def second_visit(i, carry):
            for cs, outs in advance(i):
                for c, o in zip(cs, outs):
                    finish(c, o + oh[rb(c), :])
            return carry

        half = n_groups // 2
        lax.fori_loop(0, half, first_visit, 0)
        if n_groups % 2:
            (cs_f, outs_f), (cs_b, outs_b) = advance(half)
            for c, o, ob in zip(cs_f, outs_f, reversed(outs_b)):
                finish(c, o + ob)
        lax.fori_loop(n_groups - half, n_groups, second_visit, 0)

    scan(qc, (f0c, f1c), vc, zc, yc, 0, nc)
    scan(ql, (f0l, f1l), vl, zl, yl, Lc, nl)


def _hg_mixer(ctx_parts, lat_parts, norm_w):
    B, Lc, _ = ctx_parts[0].shape
    Ll = lat_parts[0].shape[1]
    blk = lambda L: pl.BlockSpec((1, L, DH), lambda b, h: (b, 0, h))
    Lt = Lc + Ll
    return pl.pallas_call(
        functools.partial(_hg_mixer_kernel, Lc=Lc, Ll=Ll),
        grid=(B, HEADS),
        in_specs=[blk(Lc)] * 5 + [blk(Ll)] * 5 + [pl.BlockSpec((1, DH), lambda b, h: (0, 0))],
        out_specs=[blk(Lc), blk(Ll)],
        out_shape=[jax.ShapeDtypeStruct((B, Lc, HEADS * DH), BF16),
                   jax.ShapeDtypeStruct((B, Ll, HEADS * DH), BF16)],
        scratch_shapes=[pltpu.VMEM((Lt, DH), F32)] + [pltpu.VMEM((DH, DH), F32)] * 2,
        compiler_params=_cparams("parallel", "arbitrary"),
        name="hg_mixer",
    )(*ctx_parts, *lat_parts, norm_w)


def _lru_chunk_parts(x, w_ref, b_ref, sp, d):
    rev = d == 1
    cs = slice(2 * DH * d, 2 * DH * (d + 1))
    gates = _mm(x.astype(BF16), w_ref[0, :, cs]) + b_ref[0, :, cs]
    r = jax.nn.sigmoid(gates[:, 0:DH])
    ig = jax.nn.sigmoid(gates[:, DH:2 * DH])
    log_a = -LRU_C * r * sp
    a = jnp.exp(log_a)
    bv = jnp.sqrt(1.0 - jnp.exp(2.0 * log_a)) * (ig * x)
    rows = lax.broadcasted_iota(jnp.int32, (CHUNK, DH), 0)
    k = 1
    while k < CHUNK:
        if rev:
            m = rows < CHUNK - k
            a_sh, b_sh = _shift_rows(a, k), _shift_rows(bv, k)
        else:
            m = rows >= k
            a_sh, b_sh = _shift_rows(a, -k), _shift_rows(bv, -k)
        bv = jnp.where(m, bv + a * b_sh, bv)
        a = jnp.where(m, a * a_sh, a)
        k *= 2
    return a, bv


def _lru_mixer_kernel(xc, gc, xl, gl, cw, cb, w4, b4, lam, yc, yl, xs, of, ob, *, Lc, Ll):
    nc, nl = Lc // CHUNK, Ll // CHUNK
    cw_v = cw[...]
    cb_v = cb[...]

    def prep(x_ref, base, n):
        def fn(t):
            dst = pl.ds(base + t * CHUNK, CHUNK) if isinstance(t, int) else \
                pl.ds(pl.multiple_of(base + t * CHUNK, CHUNK), CHUNK)
            xs[dst, :] = _conv_chunk(x_ref, t, n, cw_v) + cb_v
        _for_chunks(n, fn)

    prep(xc, 0, nc)
    prep(xl, Lc, nl)
    sp0 = _softplus(-lam[0:1, :])
    sp1 = _softplus(-lam[1:2, :])

    def scan(base, n, carry):
        grp = min(SCAN_GROUP, n)

        def body(i, c):
            hs = list(c)
            for d, (sp, o_ref) in enumerate(((sp0, of), (sp1, ob))):
                cs = [i * grp + g for g in range(grp)] if d == 0 else \
                    [n - 1 - i * grp - g for g in range(grp)]
                parts = [_lru_chunk_parts(xs[_rows(t, base), :], w4, b4, sp, d) for t in cs]
                end = CHUNK - 1 if d == 0 else 0
                for t, (a_cum, b_cum) in zip(cs, parts):
                    hrows = b_cum + a_cum * hs[d]
                    o_ref[_rows(t, base), :] = hrows
                    hs[d] = hrows[end:end + 1, :]
            return tuple(hs)
        return lax.fori_loop(0, n // grp, body, carry)

    zero = jnp.zeros((1, DH), F32)
    carry = scan(0, nc, (zero, zero))
    scan(Lc, nl, carry)

    def out(g_ref, y_ref, base, n):
        def body(t, c):
            r = _rows(t)
            rb = pl.ds(pl.multiple_of(base + t * CHUNK, CHUNK), CHUNK)
            y_ref[0, r, :] = (g_ref[0, r, :].astype(F32) * (of[rb, :] + ob[rb, :])).astype(y_ref.dtype)
            return c
        lax.fori_loop(0, n, body, 0)

    out(gc, yc, 0, nc)
    out(gl, yl, Lc, nl)


def _lru_mixer(xr_c, gate_c, xr_l, gate_l, conv_w, conv_b, w4, b4, lam):
    B, Lc, _ = xr_c.shape
    Ll = xr_l.shape[1]
    blk = lambda L: pl.BlockSpec((1, L, DH), lambda b, g: (b, 0, g))
    Lt = Lc + Ll
    return pl.pallas_call(
        functools.partial(_lru_mixer_kernel, Lc=Lc, Ll=Ll),
        grid=(B, HEADS),
        in_specs=[blk(Lc), blk(Lc), blk(Ll), blk(Ll),
                  pl.BlockSpec((4, DH), lambda b, g: (0, g)),
                  pl.BlockSpec((1, DH), lambda b, g: (0, g)),
                  pl.BlockSpec((1, DH, 4 * DH), lambda b, g: (g, 0, 0)),
                  pl.BlockSpec((1, 1, 4 * DH), lambda b, g: (g, 0, 0)),
                  pl.BlockSpec((2, DH), lambda b, g: (0, g))],
        out_specs=[blk(Lc), blk(Ll)],
        out_shape=[jax.ShapeDtypeStruct((B, Lc, HEADS * DH), BF16),
                   jax.ShapeDtypeStruct((B, Ll, HEADS * DH), BF16)],
        scratch_shapes=[pltpu.VMEM((Lt, DH), F32)] * 3,
        compiler_params=_cparams("parallel", "arbitrary"),
        name="lru_mixer",
    )(xr_c, gate_c, xr_l, gate_l, conv_w, conv_b, w4, b4, lam)


def _to_scan_order(h):
    bsz, L, d = h.shape
    return h.reshape(bsz, L // GRID_W, GRID_W, d).transpose(0, 2, 1, 3).reshape(bsz, L, d)


def _from_scan_order(h):
    bsz, L, d = h.shape
    return h.reshape(bsz, GRID_W, L // GRID_W, d).transpose(0, 2, 1, 3).reshape(bsz, L, d)


def _pad_lanes(v, layout):
    row = jnp.zeros((DH,), F32)
    for off in layout:
        row = row.at[off:off + v.shape[0]].set(v)
    return row.reshape(1, DH)


def kernel(x, c, ctx, c_ctx, ada_w, ada_b, norm_mix, norm_ffn, norm_final, ffn_w1, ffn_w3, ffn_w2, gdn_w_in, gdn_conv, gdn_a_log, gdn_dt_bias, gdn_norm, gdn_w_out, lru_w_in, lru_conv_w, lru_conv_b, lru_w_r, lru_b_r, lru_w_i, lru_b_i, lru_lambda, lru_w_out, hg_w_in, hg_lb_logits, hg_norm, hg_w_out):
    B, L, d = x.shape
    depth = ada_w.shape[0]
    assert d == D_MODEL and L % GRID_W == 0 and L % GDN_CHUNK == 0 and ctx.shape[1] % GDN_CHUNK == 0
    assert B + 1 <= 8

    cvec = jnp.zeros((8, d), F32).at[0:B].set(c).at[B].set(c_ctx)
    mods = _mods(cvec, ada_w, ada_b).reshape(depth, 8, 6, d)

    xl, xc = x, ctx
    for i in range(depth):
        mod_l = mods[i, 0:B]
        mod_c = jnp.broadcast_to(mods[i, B][None], (B, 6, d))
        col = i % 2 == 1
        if col:
            xl = _to_scan_order(xl)
        nw = norm_mix[i].reshape(1, d)
        kind, j = i % 3, i // 3
        if kind == 0:
            w_in = gdn_w_in[j]
            n_main = 4 * HEADS * DH
            w_main = w_in[:, :n_main].astype(BF16)
            wa = w_in[:, n_main:n_main + 2 * HEADS]
            wb = w_in[:, n_main + 2 * HEADS:n_main + 4 * HEADS]
            wab = jnp.zeros((d, DH), F32)
            wab = wab.at[:, 0:2 * HEADS].set(wa).at[:, 2 * HEADS:4 * HEADS].set(wb)
            wab_hi = wab.astype(BF16)
            wab_lo = (wab - wab_hi.astype(F32)).astype(BF16)
            alog = _pad_lanes(gdn_a_log[j].reshape(-1), (0,))
            dtb = _pad_lanes(gdn_dt_bias[j].reshape(-1), (0,))
            pc, grc = _gdn_proj(xc, nw, mod_c, w_main, gdn_conv[j], wab_hi, wab_lo, alog, dtb)
            plat, grl = _gdn_proj(xl, nw, mod_l, w_main, gdn_conv[j], wab_hi, wab_lo, alog, dtb)
            yc, yl = _gdn_mixer(pc, grc, plat, grl, gdn_norm[j].reshape(1, DH))
            w_out = gdn_w_out[j]
        elif kind == 1:
            w_in = lru_w_in[j].astype(BF16)
            gate_c, xr_c = _lru_proj(xc, nw, mod_c, w_in)
            gate_l, xr_l = _lru_proj(xl, nw, mod_l, w_in)
            w4 = jnp.concatenate([lru_w_r[j, 0], lru_w_i[j, 0], lru_w_r[j, 1], lru_w_i[j, 1]],
                                 axis=-1).astype(BF16)
            b4 = jnp.stack([lru_b_r[j, 0], lru_b_i[j, 0], lru_b_r[j, 1], lru_b_i[j, 1]], axis=0)
            b4 = b4.reshape(4, HEADS, DH).transpose(1, 0, 2).reshape(HEADS, 1, 4 * DH)
            yc, yl = _lru_mixer(xr_c, gate_c, xr_l, gate_l, lru_conv_w[j],
                                lru_conv_b[j].reshape(1, d), w4, b4, lru_lambda[j])
            w_out = lru_w_out[j]
        else:
            w_in = hg_w_in[j].astype(BF16)
            parts_c = _hg_proj(xc, nw, mod_c, w_in, hg_lb_logits, i)
            parts_l = _hg_proj(xl, nw, mod_l, w_in, hg_lb_logits, i)
            yc, yl = _hg_mixer(parts_c, parts_l, hg_norm[j].reshape(1, DH))
            w_out = hg_w_out[j]

        nwf = norm_ffn[i].reshape(1, d)
        wo, w1, w3, w2 = (w.astype(BF16) for w in (w_out, ffn_w1[i], ffn_w3[i], ffn_w2[i]))
        last = i == depth - 1
        fnw = norm_final.reshape(1, d)
        xl = _post(xl, yl, mod_l, nwf, wo, w1, w3, w2, fnw, last)
        if not last:
            xc = _post(xc, yc, mod_c, nwf, wo, w1, w3, w2, fnw, False)
        if col:
            xl = _from_scan_order(xl)
    return xl
```

```python
import functools
import math

import jax
import jax.numpy as jnp
from jax import lax
from jax.experimental import pallas as pl
from jax.experimental.pallas import tpu as pltpu

F32 = jnp.float32
BF16 = jnp.bfloat16

D_MODEL = 1024
HEADS = 8
DH = 128
CHUNK = 64
GDN_CHUNK = 128
GRID_W = 64
EPS = 1e-6
LRU_C = 8.0
FFN_HIDDEN = 2816
FFN_COLS = 256
TOKEN_TILE = 512
V7X_VMEM_BYTES = 64 * 1024 * 1024
VMEM_LIMIT = V7X_VMEM_BYTES - 6 * 1024 * 1024


def _cparams(*sem):
    return pltpu.CompilerParams(dimension_semantics=sem, vmem_limit_bytes=VMEM_LIMIT)


def _mm(a, b):
    return jnp.dot(a, b, preferred_element_type=F32)


def _mm_nt(a, b):
    return lax.dot_general(a, b, (((1,), (1,)), ((), ())), preferred_element_type=F32)


def _mm_tn(a, b):
    return lax.dot_general(a, b, (((0,), (0,)), ((), ())), preferred_element_type=F32)


def _silu(x):
    return x * jax.nn.sigmoid(x)


def _softplus(x):
    return jnp.maximum(x, 0.0) + jnp.log(1.0 + jnp.exp(-jnp.abs(x)))


def _gelu_tanh(x):
    return 0.5 * x * (1.0 + jnp.tanh(math.sqrt(2.0 / math.pi) * (x + 0.044715 * (x * x * x))))


def _norm_mod(x, nw, shift, scale):
    ms = jnp.mean(x * x, axis=-1, keepdims=True)
    y = x * lax.rsqrt(ms + EPS) * nw
    return y * (1.0 + scale) + shift


def _shift_rows(x, k):
    n = x.shape[0]
    return x if k % n == 0 else pltpu.roll(x, (-k) % n, 0)


def _chunk_cumsum(x, rows_in_chunk, rev, chunk=CHUNK):
    k = 1
    while k < chunk:
        if rev:
            x = x + jnp.where(rows_in_chunk < chunk - k, _shift_rows(x, k), 0.0)
        else:
            x = x + jnp.where(rows_in_chunk >= k, _shift_rows(x, -k), 0.0)
        k *= 2
    return x


def _conv_chunk(ref, t, n_chunks, cw, chunk=CHUNK):
    first = isinstance(t, int) and t == 0
    last = isinstance(t, int) and t == n_chunks - 1
    halo = 16
    if first:
        win = ref[0, 0:chunk + halo, :].astype(F32)
        off = 0
    elif last:
        start = t * chunk - halo
        win = ref[0, start:start + chunk + halo, :].astype(F32)
        off = halo
    else:
        start = t * chunk - halo
        if not isinstance(t, int):
            start = pl.multiple_of(start, halo)
        win = ref[0, pl.ds(start, chunk + 2 * halo), :].astype(F32)
        off = halo
    rows = lax.broadcasted_iota(jnp.int32, (chunk, win.shape[1]), 0)

    def tap(k):
        return _shift_rows(win, k)[off:off + chunk]

    xm2, xm1, x0, xp1 = tap(-2), tap(-1), tap(0), tap(1)
    if first:
        xm2 = jnp.where(rows >= 2, xm2, 0.0)
        xm1 = jnp.where(rows >= 1, xm1, 0.0)
    if last:
        xp1 = jnp.where(rows < chunk - 1, xp1, 0.0)
    return cw[0:1] * xm2 + cw[1:2] * xm1 + cw[2:3] * x0 + cw[3:4] * xp1


def _for_chunks(n_chunks, fn):
    fn(0)
    if n_chunks > 2:
        def body(t, carry):
            fn(t)
            return carry
        lax.fori_loop(1, n_chunks - 1, body, 0)
    if n_chunks > 1:
        fn(n_chunks - 1)


def _rows(t, base=0, chunk=CHUNK):
    start = base + t * chunk
    if not isinstance(start, int):
        start = pl.multiple_of(start, chunk)
    return pl.ds(start, chunk)


def _gated_norm_out(of, ob, z_ref, y_ref, nw, base, n_chunks):
    def body(t, carry):
        r = _rows(t)
        rb = pl.ds(pl.multiple_of(base + t * CHUNK, CHUNK), CHUNK)
        o = of[rb, :] + ob[rb, :]
        ms = jnp.mean(o * o, axis=-1, keepdims=True)
        y = o * lax.rsqrt(ms + EPS) * nw
        y_ref[0, r, :] = (y * _silu(z_ref[0, r, :].astype(F32))).astype(y_ref.dtype)
        return carry
    lax.fori_loop(0, n_chunks, body, 0)


def _mods_kernel(cv_ref, w_ref, b_ref, o_ref):
    s = _silu(cv_ref[...])
    o_ref[0] = jnp.dot(s, w_ref[0], preferred_element_type=F32,
                       precision=lax.Precision.HIGHEST) + b_ref[0]


def _mods(cvec, ada_w, ada_b):
    depth, d, n = ada_w.shape
    tn = 1536
    return pl.pallas_call(
        _mods_kernel,
        grid=(depth, n // tn),
        in_specs=[
            pl.BlockSpec((8, d), lambda i, j: (0, 0)),
            pl.BlockSpec((1, d, tn), lambda i, j: (i, 0, j)),
            pl.BlockSpec((1, 1, tn), lambda i, j: (i, 0, j)),
        ],
        out_specs=pl.BlockSpec((1, 8, tn), lambda i, j: (i, 0, j)),
        out_shape=jax.ShapeDtypeStruct((depth, 8, n), F32),
        compiler_params=_cparams("parallel", "parallel"),
        name="adaln_mods",
    )(cvec, ada_w, ada_b.reshape(depth, 1, n))


def _token_tile(L):
    return min(TOKEN_TILE, L)


def _proj_in_specs(tm, d):
    return [
        pl.BlockSpec((1, tm, d), lambda b, t: (b, t, 0)),
        pl.BlockSpec((1, d), lambda b, t: (0, 0)),
        pl.BlockSpec((1, 6, d), lambda b, t: (b, 0, 0)),
    ]


def _full(shape):
    nd = len(shape)
    return pl.BlockSpec(shape, lambda b, t: (0,) * nd)


HALO = 8


def _gdn_proj_kernel(x_ref, xp_ref, xn_ref, nw_ref, mod_ref, w_ref, cw_ref, wabh_ref, wabl_ref,
                     alog_ref, dtb_ref, qkvz_ref, grow_ref):
    t, nt = pl.program_id(1), pl.num_programs(1)
    tm = x_ref.shape[1]
    nw, shift, scale = nw_ref[...], mod_ref[0, 0:1], mod_ref[0, 1:2]
    h_prev = _norm_mod(xp_ref[0], nw, shift, scale) * (t > 0).astype(F32)
    h_next = _norm_mod(xn_ref[0], nw, shift, scale) * (t < nt - 1).astype(F32)
    ha = jnp.concatenate([h_prev, _norm_mod(x_ref[0], nw, shift, scale), h_next], axis=0)
    hab = ha.astype(BF16)
    for c in range(3):
        cs = slice(c * D_MODEL, (c + 1) * D_MODEL)
        p = _mm(hab, w_ref[:, cs])
        cw = cw_ref[:, cs]
        y = cw[0:1] * _shift_rows(p, -2)[HALO:HALO + tm]
        y = y + cw[1:2] * _shift_rows(p, -1)[HALO:HALO + tm]
        y = y + cw[2:3] * p[HALO:HALO + tm]
        y = y + cw[3:4] * _shift_rows(p, 1)[HALO:HALO + tm]
        qkvz_ref[0, :, cs] = y.astype(qkvz_ref.dtype)
    h = ha[HALO:HALO + tm]
    hb = h.astype(BF16)
    qkvz_ref[0, :, 3 * D_MODEL:4 * D_MODEL] = \
        _mm(hb, w_ref[:, 3 * D_MODEL:4 * D_MODEL]).astype(qkvz_ref.dtype)
    hl = (h - hb.astype(F32)).astype(BF16)
    ab = _mm(hb, wabh_ref[...]) + _mm(hl, wabh_ref[...]) + _mm(hb, wabl_ref[...])
    lane = lax.broadcasted_iota(jnp.int32, ab.shape, 1)
    rin = lax.broadcasted_iota(jnp.int32, ab.shape, 0) & (GDN_CHUNK - 1)
    g = -jnp.exp(alog_ref[...]) * _softplus(ab + dtb_ref[...])
    res = jnp.where(lane < HEADS, _chunk_cumsum(g, rin, False, GDN_CHUNK),
                    jnp.where(lane < 2 * HEADS, _chunk_cumsum(g, rin, True, GDN_CHUNK),
                              jax.nn.sigmoid(ab)))
    grow_ref[0] = res.T[0:4 * HEADS, :]


def _gdn_proj(x, nw, mods, w_main, conv_w, wab_hi, wab_lo, alog, dtb):
    B, L, d = x.shape
    tm = _token_tile(L)
    n = w_main.shape[1]
    per_tile = tm // HALO
    last_blk = L // HALO - 1
    x_spec, nw_spec, mod_spec = _proj_in_specs(tm, d)
    prev_spec = pl.BlockSpec((1, HALO, d), lambda b, t: (b, jnp.maximum(t * per_tile - 1, 0), 0))
    next_spec = pl.BlockSpec((1, HALO, d),
                             lambda b, t: (b, jnp.minimum((t + 1) * per_tile, last_blk), 0))
    return pl.pallas_call(
        _gdn_proj_kernel,
        grid=(B, L // tm),
        in_specs=[x_spec, prev_spec, next_spec, nw_spec, mod_spec,
                  pl.BlockSpec(w_main.shape, lambda b, t: (0, 0), pipeline_mode=pl.Buffered(1)),
                  _full(conv_w.shape), _full(wab_hi.shape), _full(wab_lo.shape),
                  _full(alog.shape), _full(dtb.shape)],
        out_specs=[
            pl.BlockSpec((1, tm, n), lambda b, t: (b, t, 0)),
            pl.BlockSpec((1, 4 * HEADS, tm), lambda b, t: (b, 0, t)),
        ],
        out_shape=[
            jax.ShapeDtypeStruct((B, L, n), BF16),
            jax.ShapeDtypeStruct((B, 4 * HEADS, L), F32),
        ],
        compiler_params=_cparams("parallel", "parallel"),
        name="gdn_proj",
    )(x, x, x, nw, mods, w_main, conv_w, wab_hi, wab_lo, alog, dtb)


def _lru_proj_kernel(x_ref, nw_ref, mod_ref, w_ref, gate_ref, xr_ref):
    h = _norm_mod(x_ref[0], nw_ref[...], mod_ref[0, 0:1], mod_ref[0, 1:2])
    hb = h.astype(BF16)
    gate_ref[0] = _gelu_tanh(_mm(hb, w_ref[:, 0:D_MODEL])).astype(gate_ref.dtype)
    xr_ref[0] = _mm(hb, w_ref[:, D_MODEL:2 * D_MODEL])


def _lru_proj(x, nw, mods, w):
    B, L, d = x.shape
    tm = _token_tile(L)
    spec = pl.BlockSpec((1, tm, d), lambda b, t: (b, t, 0))
    return pl.pallas_call(
        _lru_proj_kernel,
        grid=(B, L // tm),
        in_specs=_proj_in_specs(tm, d) + [_full(w.shape)],
        out_specs=[spec, spec],
        out_shape=[jax.ShapeDtypeStruct((B, L, d), BF16), jax.ShapeDtypeStruct((B, L, d), F32)],
        compiler_params=_cparams("parallel", "parallel"),
        name="lru_proj",
    )(x, nw, mods, w)


def _hg_proj_kernel(x_ref, nw_ref, mod_ref, w_ref, lbl_ref, q_ref, f0_ref, f1_ref, v_ref, z_ref,
                    *, layer):
    logits = lbl_ref[...]
    depth = logits.shape[0]
    m = logits[0:1]
    for r in range(1, depth):
        m = jnp.maximum(m, logits[r:r + 1])
    e = jnp.exp(logits - m)
    tot = e[0:1]
    for r in range(1, depth):
        tot = tot + e[r:r + 1]
    lb = jnp.zeros_like(tot)
    for r in range(1, layer + 1):
        lb = lb + e[r:r + 1] / tot

    h = _norm_mod(x_ref[0], nw_ref[...], mod_ref[0, 0:1], mod_ref[0, 1:2])
    hb = h.astype(BF16)

    def cols(c):
        return _mm(hb, w_ref[:, c * D_MODEL:(c + 1) * D_MODEL])

    q_ref[0] = _silu(cols(0)).astype(q_ref.dtype)
    for d, f_ref in enumerate((f0_ref, f1_ref)):
        f = lb + (1.0 - lb) * jax.nn.sigmoid(cols(1 + d))
        f_ref[0] = jnp.log(f)
    v_ref[0] = cols(3).astype(v_ref.dtype)
    z_ref[0] = cols(4).astype(z_ref.dtype)


def _hg_proj(x, nw, mods, w, lb_logits, layer):
    B, L, d = x.shape
    tm = _token_tile(L)
    spec = pl.BlockSpec((1, tm, d), lambda b, t: (b, t, 0))
    sds = lambda dt: jax.ShapeDtypeStruct((B, L, d), dt)
    return pl.pallas_call(
        functools.partial(_hg_proj_kernel, layer=layer),
        grid=(B, L // tm),
        in_specs=_proj_in_specs(tm, d) + [_full(w.shape), _full(lb_logits.shape)],
        out_specs=[spec] * 5,
        out_shape=[sds(BF16), sds(F32), sds(F32), sds(BF16), sds(BF16)],
        compiler_params=_cparams("parallel", "parallel"),
        name="hg_proj",
    )(x, nw, mods, w, lb_logits)


def _post_kernel(x_ref, y_ref, mod_ref, nw_ref, wo_ref, w1_ref, w3_ref, w2_ref, fnw_ref,
                 o_ref, acc_ref, *, final):
    x1 = x_ref[0] + mod_ref[0, 2:3] * _mm(y_ref[0], wo_ref[...])
    hb = _norm_mod(x1, nw_ref[...], mod_ref[0, 3:4], mod_ref[0, 4:5]).astype(BF16)
    hidden = w1_ref.shape[1]
    for j in range(hidden // FFN_COLS):
        cs = slice(j * FFN_COLS, (j + 1) * FFN_COLS)
        a = (_silu(_mm(hb, w1_ref[:, cs])) * _mm(hb, w3_ref[:, cs])).astype(BF16)
        part = _mm(a, w2_ref[cs, :])
        if j == 0:
            acc_ref[...] = part
        else:
            acc_ref[...] += part
    out = x1 + mod_ref[0, 5:6] * acc_ref[...]
    if final:
        out = out * lax.rsqrt(jnp.mean(out * out, axis=-1, keepdims=True) + EPS) * fnw_ref[...]
    o_ref[0] = out


def _post(x, y, mods, nw, wo, w1, w3, w2, final_nw, final):
    B, L, d = x.shape
    tm = _token_tile(L)
    tile = pl.BlockSpec((1, tm, d), lambda b, t: (b, t, 0))

    def resident(shape):
        return pl.BlockSpec(shape, lambda b, t: (0, 0), pipeline_mode=pl.Buffered(1))

    return pl.pallas_call(
        functools.partial(_post_kernel, final=final),
        grid=(B, L // tm),
        in_specs=[tile, tile,
                  pl.BlockSpec((1, 6, d), lambda b, t: (b, 0, 0)),
                  pl.BlockSpec((1, d), lambda b, t: (0, 0)),
                  resident(wo.shape), resident(w1.shape), resident(w3.shape), resident(w2.shape),
                  pl.BlockSpec((1, d), lambda b, t: (0, 0))],
        out_specs=tile,
        out_shape=jax.ShapeDtypeStruct((B, L, d), F32),
        scratch_shapes=[pltpu.VMEM((tm, d), F32)],
        compiler_params=_cparams("parallel", "parallel"),
        name="post_ffn",
    )(x, y, mods, nw, wo, w1, w3, w2, final_nw)


GC = GDN_CHUNK
GDN_GROUP = 8


def _block_diag(m):
    left = lax.broadcasted_iota(jnp.int32, m.shape, 1) < GC
    return jnp.concatenate([jnp.where(left, m, 0.0), jnp.where(left, 0.0, m)], axis=0).astype(BF16)


def _dup_diag(m16):
    z = jnp.zeros_like(m16)
    return jnp.concatenate([jnp.concatenate([m16, z], axis=1),
                            jnp.concatenate([z, m16], axis=1)], axis=0)


def _packed_masks():
    ii = lax.broadcasted_iota(jnp.int32, (GC, 2 * GC), 0)
    jj = lax.broadcasted_iota(jnp.int32, (GC, 2 * GC), 1)
    jl = jj & (GC - 1)
    bwd = jj >= GC
    fwd = jnp.logical_not(bwd)
    incl = (fwd & (ii >= jl)) | (bwd & (ii <= jl))
    strict = (fwd & (ii > jl)) | (bwd & (ii < jl))
    same = {b: (ii & -b) == (jl & -b) for b in (4, 8, 16, 32, 64, 128) if b <= GC}
    return dict(left=fwd, incl=incl, strict=strict, eye=ii == jl, same=same)


def _tri_inverse_packed(nms, masks):
    same = masks["same"]
    n4 = [jnp.where(same[4], n, 0.0) for n in nms]
    n4d = [_block_diag(a) for a in n4]
    p2 = [_mm(a.astype(BF16), d) for a, d in zip(n4, n4d)]
    p3 = [_mm(p.astype(BF16), d) for p, d in zip(p2, n4d)]
    eye = jnp.where(masks["eye"], 1.0, 0.0)
    xs = [eye + a + b + c for a, b, c in zip(n4, p2, p3)]
    b = 4
    while b < GC:
        off_block = same[2 * b] & jnp.logical_not(same[b])
        cms = [jnp.where(off_block, n, 0.0).astype(BF16) for n in nms]
        xds = [_block_diag(x) for x in xs]
        yds = [_block_diag(_mm(c, d)) for c, d in zip(cms, xds)]
        xs = [x + _mm(x.astype(BF16), d) for x, d in zip(xs, yds)]
        b *= 2
    return xs


def _gdn_precompute(q_ref, k_ref, v_ref, gp_ref, base, n_chunks,
                    mp_s, cp_s, qp_s, op_s, dl_s):
    masks = _packed_masks()
    left, incl, strict = masks["left"], masks["incl"], masks["strict"]
    grp = min(GDN_GROUP, n_chunks)

    def group(t0):
        ts = [t0 + g for g in range(grp)]
        tiles = [gp_ref[0, 0, t] for t in ts]
        cols = [tl.T for tl in tiles]
        qs, ks, v16s = [], [], []
        for t in ts:
            r = _rows(t, 0, GC)
            q = _silu(q_ref[0, r, :].astype(F32))
            qs.append(q * lax.rsqrt(jnp.sum(q * q, axis=-1, keepdims=True) + EPS) * (DH ** -0.5))
            k = _silu(k_ref[0, r, :].astype(F32))
            ks.append(k * lax.rsqrt(jnp.sum(k * k, axis=-1, keepdims=True) + EPS))
            v16s.append(_silu(v_ref[0, r, :].astype(F32)).astype(BF16))
        k16s = [k.astype(BF16) for k in ks]
        q16s = [q.astype(BF16) for q in qs]
        kqs = [_mm_nt(jnp.concatenate([k, q], axis=0), jnp.concatenate([k, k], axis=0))
               for k, q in zip(k16s, q16s)]
        g_rows = [tl[0:1, :] for tl in tiles]
        b_rows = [tl[1:2, :] for tl in tiles]
        g_cols = [jnp.where(left, c[0:GC, 0:1], c[GC:2 * GC, 0:1]) for c in cols]
        b_cols = [jnp.where(left, c[0:GC, 1:2], c[GC:2 * GC, 1:2]) for c in cols]
        decays = [jnp.where(incl, jnp.exp(jnp.where(incl, gc - gr, 0.0)), 0.0)
                  for gc, gr in zip(g_cols, g_rows)]
        nms = [jnp.where(strict, -(kq[0:GC] * bc * dc), 0.0)
               for kq, bc, dc in zip(kqs, b_cols, decays)]
        tinvs = _tri_inverse_packed(nms, masks)
        tus = [(x * br).astype(BF16) for x, br in zip(tinvs, b_rows)]
        tws = [(x * (br * jnp.exp(gr))).astype(BF16) for x, br, gr in zip(tinvs, b_rows, g_rows)]
        uus = [_mm(tu, _dup_diag(v)) for tu, v in zip(tus, v16s)]
        wws = [_mm(tw, _dup_diag(k)) for tw, k in zip(tws, k16s)]
        u16s = [u.astype(BF16) for u in uus]
        w16s = [w.astype(BF16) for w in wws]
        a16s = [jnp.where(incl, kq[GC:2 * GC] * dc, 0.0).astype(BF16)
                for kq, dc in zip(kqs, decays)]
        zero = jnp.zeros((GC, DH), BF16)
        aus = [_mm(a, jnp.concatenate(
            [jnp.concatenate([w[:, 0:DH], u[:, 0:DH], zero, zero], axis=1),
             jnp.concatenate([zero, zero, w[:, DH:2 * DH], u[:, DH:2 * DH]], axis=1)], axis=0))
            for a, w, u in zip(a16s, w16s, u16s)]
        g_tots = [(tl[0:1, GC - 1:GC], tl[0:1, GC:GC + 1]) for tl in tiles]
        g_colds = [(c[0:GC, 0:1], c[GC:2 * GC, 0:1]) for c in cols]
        kes = [[(k * jnp.exp(gt[d] - gc[d])).astype(BF16) for d in range(2)]
               for k, gt, gc in zip(ks, g_tots, g_colds)]
        kwus = [[_mm_tn(ke[d], jnp.concatenate([w[:, d * DH:(d + 1) * DH],
                                                u[:, d * DH:(d + 1) * DH]], axis=1))
                 for d in range(2)] for ke, w, u in zip(kes, w16s, u16s)]
        for g, t in enumerate(ts):
            dst = _rows(t, base, GC)
            ci = base // GC + t
            dst2 = pl.ds(ci * DH, DH) if isinstance(ci, int) else \
                pl.ds(pl.multiple_of(ci * DH, DH), DH)
            for d in range(2):
                qe = qs[g] * jnp.exp(g_colds[g][d])
                qp_s[d, dst, :] = (qe - aus[g][:, 2 * d * DH:(2 * d + 1) * DH]).astype(BF16)
                op_s[d, dst, :] = aus[g][:, (2 * d + 1) * DH:(2 * d + 2) * DH]
                mp_s[d, dst2, :] = kwus[g][d][:, 0:DH].astype(BF16)
                cp_s[d, dst2, :] = kwus[g][d][:, DH:2 * DH].astype(BF16)
                dl_s[d, pl.ds(ci, 1), :] = jnp.broadcast_to(jnp.exp(g_tots[g][d]), (1, DH))

    n_groups = n_chunks // grp
    group(0)
    if n_groups > 2:
        def body(i, carry):
            group(i * grp)
            return carry
        lax.fori_loop(1, n_groups - 1, body, 0)
    if n_groups > 1:
        group((n_groups - 1) * grp)


def _gdn_mixer_kernel(qc, kc, vc, zc, gpc, ql, kl, vl, zl, gpl, nw, yc, yl,
                      mp_s, cp_s, qp_s, op_s, dl_s, oh, sf, sb, *, Lc, Ll):
    nc, nl = Lc // GC, Ll // GC
    scr = (mp_s, cp_s, qp_s, op_s, dl_s)
    _gdn_precompute(qc, kc, vc, gpc, 0, nc, *scr)
    _gdn_precompute(ql, kl, vl, gpl, Lc, nl, *scr)
    sf[...] = jnp.zeros_like(sf)
    sb[...] = jnp.zeros_like(sb)
    nw_v = nw[...]
    s_refs = (sf, sb)

    def scan(z_ref, y_ref, base, n):
        def steps(cs):
            rbs = [_rows(c, base, GC) for c in cs]
            cis = [base // GC + c for c in cs]
            rms = [_rows(ci, 0, DH) for ci in cis]
            ss = [r[...] for r in s_refs]
            res = [_mm(jnp.concatenate([mp_s[d, rms[d], :], qp_s[d, rbs[d], :]], axis=0),
                       ss[d].astype(BF16)) for d in range(2)]
            outs = [res[d][DH:DH + GC] + op_s[d, rbs[d], :] for d in range(2)]
            for d in range(2):
                dl = dl_s[d, pl.ds(cis[d], 1), :]
                s_refs[d][...] = ss[d] * dl - res[d][0:DH] + cp_s[d, rms[d], :].astype(F32)
            return rbs, outs

        def first_visit(cs):
            rbs, outs = steps(cs)
            for d in range(2):
                oh[rbs[d], :] = outs[d]

        def second_visit(cs):
            rbs, outs = steps(cs)
            for d in range(2):
                oh[rbs[d], :] = outs[d] + oh[rbs[d], :]

        def finish(cs):
            for c in cs:
                o = oh[_rows(c, base, GC), :]
                y = o * lax.rsqrt(jnp.mean(o * o, axis=-1, keepdims=True) + EPS) * nw_v
                r = _rows(c, 0, GC)
                y_ref[0, r, :] = (y * _silu(z_ref[0, r, :].astype(F32))).astype(y_ref.dtype)

        def first_body(i, carry):
            first_visit((i, n - 1 - i))
            return carry

        def second_body(i, carry):
            finish((i - 1, n - i))
            second_visit((i, n - 1 - i))
            return carry

        half = n // 2
        lax.fori_loop(0, half, first_body, 0)
        second_visit((half, n - 1 - half))
        lax.fori_loop(half + 1, n, second_body, 0)
        finish((n - 1, 0))

    scan(zc, yc, 0, nc)
    scan(zl, yl, Lc, nl)


def _gdn_mixer(pc, grc, plat, grl, norm_w):
    B, Lc, _ = pc.shape
    Ll = plat.shape[1]

    def pack_rows(gr, L):
        g = gr.reshape(B, 2, 2, HEADS, L // GC, GC).transpose(0, 3, 4, 1, 2, 5)
        g = g.reshape(B, HEADS, L // GC, 2, 2 * GC)
        return jnp.pad(g, ((0, 0), (0, 0), (0, 0), (0, 6), (0, 0)))

    def seq_specs(L, mode):
        head_blk = lambda k: pl.BlockSpec((1, L, DH), lambda b, h, k=k: (b, 0, k * HEADS + h),
                                          pipeline_mode=mode)
        return [head_blk(0), head_blk(1), head_blk(2), head_blk(3),
                pl.BlockSpec((1, 1, L // GC, 8, 2 * GC), lambda b, h: (b, h, 0, 0, 0))]

    y_spec = lambda L: pl.BlockSpec((1, L, DH), lambda b, h: (b, 0, h))
    Lt = Lc + Ll
    per_dir = lambda w, dt: pltpu.VMEM((2, Lt, w), dt)
    return pl.pallas_call(
        functools.partial(_gdn_mixer_kernel, Lc=Lc, Ll=Ll),
        grid=(B, HEADS),
        in_specs=seq_specs(Lc, None) + seq_specs(Ll, pl.Buffered(1)) + [
            pl.BlockSpec((1, DH), lambda b, h: (0, 0))],
        out_specs=[y_spec(Lc), y_spec(Ll)],
        out_shape=[jax.ShapeDtypeStruct((B, Lc, HEADS * DH), BF16),
                   jax.ShapeDtypeStruct((B, Ll, HEADS * DH), BF16)],
        scratch_shapes=[pltpu.VMEM((2, Lt // GC * DH, DH), BF16),
                        pltpu.VMEM((2, Lt // GC * DH, DH), BF16),
                        per_dir(DH, BF16), per_dir(DH, F32),
                        pltpu.VMEM((2, Lt // GC, DH), F32),
                        pltpu.VMEM((Lt, DH), F32), pltpu.VMEM((DH, DH), F32),
                        pltpu.VMEM((DH, DH), F32)],
        compiler_params=_cparams("parallel", "arbitrary"),
        name="gdn_mixer",
    )(pc, pc, pc, pc, pack_rows(grc, Lc), plat, plat, plat, plat, pack_rows(grl, Ll),
      norm_w)


SCAN_GROUP = 8


def _hg_chunk_parts(q16, logf, v16, rev):
    rows = lax.broadcasted_iota(jnp.int32, (CHUNK, DH), 0)
    gcum = _chunk_cumsum(logf, rows, rev)
    mid = CHUNK // 2 - 1 if rev else CHUNK // 2
    end = 0 if rev else CHUNK - 1
    ref = gcum[mid:mid + 1, :]
    glast = gcum[end:end + 1, :]
    q = q16.astype(F32)
    k = 1.0 - jnp.exp(logf)
    ii = lax.broadcasted_iota(jnp.int32, (CHUNK, CHUNK), 0)
    jj = lax.broadcasted_iota(jnp.int32, (CHUNK, CHUNK), 1)
    incl = (ii <= jj) if rev else (ii >= jj)
    scores = _mm_nt((q * jnp.exp(gcum - ref)).astype(BF16), (k * jnp.exp(ref - gcum)).astype(BF16))
    intra = _mm(jnp.where(incl, scores, 0.0).astype(BF16), v16)
    qd = (q * jnp.exp(gcum)).astype(BF16)
    upd = _mm_tn(v16, (k * jnp.exp(glast - gcum)).astype(BF16))
    return intra, qd, upd, jnp.exp(glast)


def _hg_mixer_kernel(qc, f0c, f1c, vc, zc, ql, f0l, f1l, vl, zl, nw, yc, yl, oh, sf, sb,
                     *, Lc, Ll):
    nc, nl = Lc // CHUNK, Ll // CHUNK
    sf[...] = jnp.zeros_like(sf)
    sb[...] = jnp.zeros_like(sb)
    nw_v = nw[...]

    def scan(q_ref, f_refs, v_ref, z_ref, y_ref, base, n):
        grp = min(SCAN_GROUP, n)
        n_groups = n // grp

        def advance(i):
            res = []
            for d, s_ref in enumerate((sf, sb)):
                cs = [i * grp + g for g in range(grp)] if d == 0 else \
                    [n - 1 - i * grp - g for g in range(grp)]
                parts = [_hg_chunk_parts(q_ref[0, _rows(c), :], f_refs[d][0, _rows(c), :],
                                         v_ref[0, _rows(c), :], d == 1) for c in cs]
                st = s_ref[...]
                outs = []
                for intra, qd, upd, dl in parts:
                    outs.append(intra + _mm_nt(qd, st.astype(BF16)))
                    st = st * dl + upd
                s_ref[...] = st
                res.append((cs, outs))
            return res

        def finish(c, o):
            y = o * lax.rsqrt(jnp.mean(o * o, axis=-1, keepdims=True) + EPS) * nw_v
            r = _rows(c)
            y_ref[0, r, :] = (y * _silu(z_ref[0, r, :].astype(F32))).astype(y_ref.dtype)

        def rb(c):
            return _rows(c, base)

        def first_visit(i, carry):
            for cs, outs in advance(i):
                for c, o in zip(cs, outs):
                    oh[rb(c), :] = o
            return carry

        def second_visit(i, carry):
            for cs, outs in advance(i):
                for c, o in zip(cs, outs):
                    finish(c, o + oh[rb(c), :])
            return carry

        half = n_groups // 2
        lax.fori_loop(0, half, first_visit, 0)
        if n_groups % 2:
            (cs_f, outs_f), (cs_b, outs_b) = advance(half)
            for c, o, ob in zip(cs_f, outs_f, reversed(outs_b)):
                finish(c, o + ob)
        lax.fori_loop(n_groups - half, n_groups, second_visit, 0)

    scan(qc, (f0c, f1c), vc, zc, yc, 0, nc)
    scan(ql, (f0l, f1l), vl, zl, yl, Lc, nl)


def _hg_mixer(ctx_parts, lat_parts, norm_w):
    B, Lc, _ = ctx_parts[0].shape
    Ll = lat_parts[0].shape[1]
    blk = lambda L: pl.BlockSpec((1, L, DH), lambda b, h: (b, 0, h))
    Lt = Lc + Ll
    return pl.pallas_call(
        functools.partial(_hg_mixer_kernel, Lc=Lc, Ll=Ll),
        grid=(B, HEADS),
        in_specs=[blk(Lc)] * 5 + [blk(Ll)] * 5 + [pl.BlockSpec((1, DH), lambda b, h: (0, 0))],
        out_specs=[blk(Lc), blk(Ll)],
        out_shape=[jax.ShapeDtypeStruct((B, Lc, HEADS * DH), BF16),
                   jax.ShapeDtypeStruct((B, Ll, HEADS * DH), BF16)],
        scratch_shapes=[pltpu.VMEM((Lt, DH), F32)] + [pltpu.VMEM((DH, DH), F32)] * 2,
        compiler_params=_cparams("parallel", "arbitrary"),
        name="hg_mixer",
    )(*ctx_parts, *lat_parts, norm_w)


def _lru_chunk_parts(x, w_ref, b_ref, sp, d):
    rev = d == 1
    cs = slice(2 * DH * d, 2 * DH * (d + 1))
    gates = _mm(x.astype(BF16), w_ref[0, :, cs]) + b_ref[0, :, cs]
    r = jax.nn.sigmoid(gates[:, 0:DH])
    ig = jax.nn.sigmoid(gates[:, DH:2 * DH])
    log_a = -LRU_C * r * sp
    a = jnp.exp(log_a)
    bv = jnp.sqrt(1.0 - jnp.exp(2.0 * log_a)) * (ig * x)
    rows = lax.broadcasted_iota(jnp.int32, (CHUNK, DH), 0)
    k = 1
    while k < CHUNK:
        if rev:
            m = rows < CHUNK - k
            a_sh, b_sh = _shift_rows(a, k), _shift_rows(bv, k)
        else:
            m = rows >= k
            a_sh, b_sh = _shift_rows(a, -k), _shift_rows(bv, -k)
        bv = jnp.where(m, bv + a * b_sh, bv)
        a = jnp.where(m, a * a_sh, a)
        k *= 2
    return a, bv


def _lru_mixer_kernel(xc, gc, xl, gl, cw, cb, w4, b4, lam, yc, yl, xs, of, ob, *, Lc, Ll):
    nc, nl = Lc // CHUNK, Ll // CHUNK
    cw_v = cw[...]
    cb_v = cb[...]

    def prep(x_ref, base, n):
        def fn(t):
            dst = pl.ds(base + t * CHUNK, CHUNK) if isinstance(t, int) else \
                pl.ds(pl.multiple_of(base + t * CHUNK, CHUNK), CHUNK)
            xs[dst, :] = _conv_chunk(x_ref, t, n, cw_v) + cb_v
        _for_chunks(n, fn)

    prep(xc, 0, nc)
    prep(xl, Lc, nl)
    sp0 = _softplus(-lam[0:1, :])
    sp1 = _softplus(-lam[1:2, :])

    def scan(base, n, carry):
        grp = min(SCAN_GROUP, n)

        def body(i, c):
            hs = list(c)
            for d, (sp, o_ref) in enumerate(((sp0, of), (sp1, ob))):
                cs = [i * grp + g for g in range(grp)] if d == 0 else \
                    [n - 1 - i * grp - g for g in range(grp)]
                parts = [_lru_chunk_parts(xs[_rows(t, base), :], w4, b4, sp, d) for t in cs]
                end = CHUNK - 1 if d == 0 else 0
                for t, (a_cum, b_cum) in zip(cs, parts):
                    hrows = b_cum + a_cum * hs[d]
                    o_ref[_rows(t, base), :] = hrows
                    hs[d] = hrows[end:end + 1, :]
            return tuple(hs)
        return lax.fori_loop(0, n // grp, body, carry)

    zero = jnp.zeros((1, DH), F32)
    carry = scan(0, nc, (zero, zero))
    scan(Lc, nl, carry)

    def out(g_ref, y_ref, base, n):
        def body(t, c):
            r = _rows(t)
            rb = pl.ds(pl.multiple_of(base + t * CHUNK, CHUNK), CHUNK)
            y_ref[0, r, :] = (g_ref[0, r, :].astype(F32) * (of[rb, :] + ob[rb, :])).astype(y_ref.dtype)
            return c
        lax.fori_loop(0, n, body, 0)

    out(gc, yc, 0, nc)
    out(gl, yl, Lc, nl)


def _lru_mixer(xr_c, gate_c, xr_l, gate_l, conv_w, conv_b, w4, b4, lam):
    B, Lc, _ = xr_c.shape
    Ll = xr_l.shape[1]
    blk = lambda L: pl.BlockSpec((1, L, DH), lambda b, g: (b, 0, g))
    Lt = Lc + Ll
    return pl.pallas_call(
        functools.partial(_lru_mixer_kernel, Lc=Lc, Ll=Ll),
        grid=(B, HEADS),
        in_specs=[blk(Lc), blk(Lc), blk(Ll), blk(Ll),
                  pl.BlockSpec((4, DH), lambda b, g: (0, g)),
                  pl.BlockSpec((1, DH), lambda b, g: (0, g)),
                  pl.BlockSpec((1, DH, 4 * DH), lambda b, g: (g, 0, 0)),
                  pl.BlockSpec((1, 1, 4 * DH), lambda b, g: (g, 0, 0)),
                  pl.BlockSpec((2, DH), lambda b, g: (0, g))],
        out_specs=[blk(Lc), blk(Ll)],
        out_shape=[jax.ShapeDtypeStruct((B, Lc, HEADS * DH), BF16),
                   jax.ShapeDtypeStruct((B, Ll, HEADS * DH), BF16)],
        scratch_shapes=[pltpu.VMEM((Lt, DH), F32)] * 3,
        compiler_params=_cparams("parallel", "arbitrary"),
        name="lru_mixer",
    )(xr_c, gate_c, xr_l, gate_l, conv_w, conv_b, w4, b4, lam)


def _to_scan_order(h):
    bsz, L, d = h.shape
    return h.reshape(bsz, L // GRID_W, GRID_W, d).transpose(0, 2, 1, 3).reshape(bsz, L, d)


def _from_scan_order(h):
    bsz, L, d = h.shape
    return h.reshape(bsz, GRID_W, L // GRID_W, d).transpose(0, 2, 1, 3).reshape(bsz, L, d)


def _pad_lanes(v, layout):
    row = jnp.zeros((DH,), F32)
    for off in layout:
        row = row.at[off:off + v.shape[0]].set(v)
    return row.reshape(1, DH)


def kernel(x, c, ctx, c_ctx, ada_w, ada_b, norm_mix, norm_ffn, norm_final, ffn_w1, ffn_w3, ffn_w2, gdn_w_in, gdn_conv, gdn_a_log, gdn_dt_bias, gdn_norm, gdn_w_out, lru_w_in, lru_conv_w, lru_conv_b, lru_w_r, lru_b_r, lru_w_i, lru_b_i, lru_lambda, lru_w_out, hg_w_in, hg_lb_logits, hg_norm, hg_w_out):
    B, L, d = x.shape
    depth = ada_w.shape[0]
    assert d == D_MODEL and L % GRID_W == 0 and L % GDN_CHUNK == 0 and ctx.shape[1] % GDN_CHUNK == 0
    assert B + 1 <= 8

    cvec = jnp.zeros((8, d), F32).at[0:B].set(c).at[B].set(c_ctx)
    mods = _mods(cvec, ada_w, ada_b).reshape(depth, 8, 6, d)

    xl, xc = x, ctx
    for i in range(depth):
        mod_l = mods[i, 0:B]
        mod_c = jnp.broadcast_to(mods[i, B][None], (B, 6, d))
        col = i % 2 == 1
        if col:
            xl = _to_scan_order(xl)
        nw = norm_mix[i].reshape(1, d)
        kind, j = i % 3, i // 3
        if kind == 0:
            w_in = gdn_w_in[j]
            n_main = 4 * HEADS * DH
            w_main = w_in[:, :n_main].astype(BF16)
            wa = w_in[:, n_main:n_main + 2 * HEADS]
            wb = w_in[:, n_main + 2 * HEADS:n_main + 4 * HEADS]
            wab = jnp.zeros((d, DH), F32)
            wab = wab.at[:, 0:2 * HEADS].set(wa).at[:, 2 * HEADS:4 * HEADS].set(wb)
            wab_hi = wab.astype(BF16)
            wab_lo = (wab - wab_hi.astype(F32)).astype(BF16)
            alog = _pad_lanes(gdn_a_log[j].reshape(-1), (0,))
            dtb = _pad_lanes(gdn_dt_bias[j].reshape(-1), (0,))
            pc, grc = _gdn_proj(xc, nw, mod_c, w_main, gdn_conv[j], wab_hi, wab_lo, alog, dtb)
            plat, grl = _gdn_proj(xl, nw, mod_l, w_main, gdn_conv[j], wab_hi, wab_lo, alog, dtb)
            yc, yl = _gdn_mixer(pc, grc, plat, grl, gdn_norm[j].reshape(1, DH))
            w_out = gdn_w_out[j]
        elif kind == 1:
            w_in = lru_w_in[j].astype(BF16)
            gate_c, xr_c = _lru_proj(xc, nw, mod_c, w_in)
            gate_l, xr_l = _lru_proj(xl, nw, mod_l, w_in)
            w4 = jnp.concatenate([lru_w_r[j, 0], lru_w_i[j, 0], lru_w_r[j, 1], lru_w_i[j, 1]],
                                 axis=-1).astype(BF16)
            b4 = jnp.stack([lru_b_r[j, 0], lru_b_i[j, 0], lru_b_r[j, 1], lru_b_i[j, 1]], axis=0)
            b4 = b4.reshape(4, HEADS, DH).transpose(1, 0, 2).reshape(HEADS, 1, 4 * DH)
            yc, yl = _lru_mixer(xr_c, gate_c, xr_l, gate_l, lru_conv_w[j],
                                lru_conv_b[j].reshape(1, d), w4, b4, lru_lambda[j])
            w_out = lru_w_out[j]
        else:
            w_in = hg_w_in[j].astype(BF16)
            parts_c = _hg_proj(xc, nw, mod_c, w_in, hg_lb_logits, i)
            parts_l = _hg_proj(xl, nw, mod_l, w_in, hg_lb_logits, i)
            yc, yl = _hg_mixer(parts_c, parts_l, hg_norm[j].reshape(1, DH))
            w_out = hg_w_out[j]

        nwf = norm_ffn[i].reshape(1, d)
        wo, w1, w3, w2 = (w.astype(BF16) for w in (w_out, ffn_w1[i], ffn_w3[i], ffn_w2[i]))
        last = i == depth - 1
        fnw = norm_final.reshape(1, d)
        xl = _post(xl, yl, mod_l, nwf, wo, w1, w3, w2, fnw, last)
        if not last:
            xc = _post(xc, yc, mod_c, nwf, wo, w1, w3, w2, fnw, False)
        if col:
            xl = _from_scan_order(xl)
    return xl
```

```python
import functools
import math

import jax
import jax.numpy as jnp
from jax import lax
from jax.experimental import pallas as pl
from jax.experimental.pallas import tpu as pltpu

F32 = jnp.float32
BF16 = jnp.bfloat16

D_MODEL = 1024
HEADS = 8
DH = 128
CHUNK = 64
GDN_CHUNK = 128
GRID_W = 64
EPS = 1e-6
LRU_C = 8.0
FFN_HIDDEN = 2816
FFN_COLS = 256
TOKEN_TILE = 512
V7X_VMEM_BYTES = 64 * 1024 * 1024
VMEM_LIMIT = V7X_VMEM_BYTES - 6 * 1024 * 1024


def _cparams(*sem):
    return pltpu.CompilerParams(dimension_semantics=sem, vmem_limit_bytes=VMEM_LIMIT)


def _mm(a, b):
    return jnp.dot(a, b, preferred_element_type=F32)


def _mm_nt(a, b):
    return lax.dot_general(a, b, (((1,), (1,)), ((), ())), preferred_element_type=F32)


def _mm_tn(a, b):
    return lax.dot_general(a, b, (((0,), (0,)), ((), ())), preferred_element_type=F32)


def _silu(x):
    return x * jax.nn.sigmoid(x)


def _softplus(x):
    return jnp.maximum(x, 0.0) + jnp.log(1.0 + jnp.exp(-jnp.abs(x)))


def _gelu_tanh(x):
    return 0.5 * x * (1.0 + jnp.tanh(math.sqrt(2.0 / math.pi) * (x + 0.044715 * (x * x * x))))


def _norm_mod(x, nw, shift, scale):
    ms = jnp.mean(x * x, axis=-1, keepdims=True)
    y = x * lax.rsqrt(ms + EPS) * nw
    return y * (1.0 + scale) + shift


def _shift_rows(x, k):
    n = x.shape[0]
    return x if k % n == 0 else pltpu.roll(x, (-k) % n, 0)


def _chunk_cumsum(x, rows_in_chunk, rev, chunk=CHUNK):
    k = 1
    while k < chunk:
        if rev:
            x = x + jnp.where(rows_in_chunk < chunk - k, _shift_rows(x, k), 0.0)
        else:
            x = x + jnp.where(rows_in_chunk >= k, _shift_rows(x, -k), 0.0)
        k *= 2
    return x


def _conv_chunk(ref, t, n_chunks, cw, chunk=CHUNK):
    first = isinstance(t, int) and t == 0
    last = isinstance(t, int) and t == n_chunks - 1
    halo = 16
    if first:
        win = ref[0, 0:chunk + halo, :].astype(F32)
        off = 0
    elif last:
        start = t * chunk - halo
        win = ref[0, start:start + chunk + halo, :].astype(F32)
        off = halo
    else:
        start = t * chunk - halo
        if not isinstance(t, int):
            start = pl.multiple_of(start, halo)
        win = ref[0, pl.ds(start, chunk + 2 * halo), :].astype(F32)
        off = halo
    rows = lax.broadcasted_iota(jnp.int32, (chunk, win.shape[1]), 0)

    def tap(k):
        return _shift_rows(win, k)[off:off + chunk]

    xm2, xm1, x0, xp1 = tap(-2), tap(-1), tap(0), tap(1)
    if first:
        xm2 = jnp.where(rows >= 2, xm2, 0.0)
        xm1 = jnp.where(rows >= 1, xm1, 0.0)
    if last:
        xp1 = jnp.where(rows < chunk - 1, xp1, 0.0)
    return cw[0:1] * xm2 + cw[1:2] * xm1 + cw[2:3] * x0 + cw[3:4] * xp1


def _for_chunks(n_chunks, fn):
    fn(0)
    if n_chunks > 2:
        def body(t, carry):
            fn(t)
            return carry
        lax.fori_loop(1, n_chunks - 1, body, 0)
    if n_chunks > 1:
        fn(n_chunks - 1)


def _rows(t, base=0, chunk=CHUNK):
    start = base + t * chunk
    if not isinstance(start, int):
        start = pl.multiple_of(start, chunk)
    return pl.ds(start, chunk)


def _gated_norm_out(of, ob, z_ref, y_ref, nw, base, n_chunks):
    def body(t, carry):
        r = _rows(t)
        rb = pl.ds(pl.multiple_of(base + t * CHUNK, CHUNK), CHUNK)
        o = of[rb, :] + ob[rb, :]
        ms = jnp.mean(o * o, axis=-1, keepdims=True)
        y = o * lax.rsqrt(ms + EPS) * nw
        y_ref[0, r, :] = (y * _silu(z_ref[0, r, :].astype(F32))).astype(y_ref.dtype)
        return carry
    lax.fori_loop(0, n_chunks, body, 0)


def _mods_kernel(cv_ref, w_ref, b_ref, o_ref):
    s = _silu(cv_ref[...])
    o_ref[0] = jnp.dot(s, w_ref[0], preferred_element_type=F32,
                       precision=lax.Precision.HIGHEST) + b_ref[0]


def _mods(cvec, ada_w, ada_b):
    depth, d, n = ada_w.shape
    tn = 1536
    return pl.pallas_call(
        _mods_kernel,
        grid=(depth, n // tn),
        in_specs=[
            pl.BlockSpec((8, d), lambda i, j: (0, 0)),
            pl.BlockSpec((1, d, tn), lambda i, j: (i, 0, j)),
            pl.BlockSpec((1, 1, tn), lambda i, j: (i, 0, j)),
        ],
        out_specs=pl.BlockSpec((1, 8, tn), lambda i, j: (i, 0, j)),
        out_shape=jax.ShapeDtypeStruct((depth, 8, n), F32),
        compiler_params=_cparams("parallel", "parallel"),
        name="adaln_mods",
    )(cvec, ada_w, ada_b.reshape(depth, 1, n))


def _token_tile(L):
    return min(TOKEN_TILE, L)


def _proj_in_specs(tm, d):
    return [
        pl.BlockSpec((1, tm, d), lambda b, t: (b, t, 0)),
        pl.BlockSpec((1, d), lambda b, t: (0, 0)),
        pl.BlockSpec((1, 6, d), lambda b, t: (b, 0, 0)),
    ]


def _full(shape):
    nd = len(shape)
    return pl.BlockSpec(shape, lambda b, t: (0,) * nd)


HALO = 8


def _gdn_proj_kernel(x_ref, xp_ref, xn_ref, nw_ref, mod_ref, w_ref, cw_ref, wabh_ref, wabl_ref,
                     alog_ref, dtb_ref, qkvz_ref, grow_ref):
    t, nt = pl.program_id(1), pl.num_programs(1)
    tm = x_ref.shape[1]
    nw, shift, scale = nw_ref[...], mod_ref[0, 0:1], mod_ref[0, 1:2]
    h_prev = _norm_mod(xp_ref[0], nw, shift, scale) * (t > 0).astype(F32)
    h_next = _norm_mod(xn_ref[0], nw, shift, scale) * (t < nt - 1).astype(F32)
    ha = jnp.concatenate([h_prev, _norm_mod(x_ref[0], nw, shift, scale), h_next], axis=0)
    hab = ha.astype(BF16)
    for c in range(3):
        cs = slice(c * D_MODEL, (c + 1) * D_MODEL)
        p = _mm(hab, w_ref[:, cs])
        cw = cw_ref[:, cs]
        y = cw[0:1] * _shift_rows(p, -2)[HALO:HALO + tm]
        y = y + cw[1:2] * _shift_rows(p, -1)[HALO:HALO + tm]
        y = y + cw[2:3] * p[HALO:HALO + tm]
        y = y + cw[3:4] * _shift_rows(p, 1)[HALO:HALO + tm]
        for hd in range(HEADS):
            qkvz_ref[0, c * HEADS + hd] = y[:, hd * DH:(hd + 1) * DH].astype(qkvz_ref.dtype)
    h = ha[HALO:HALO + tm]
    hb = h.astype(BF16)
    z = _mm(hb, w_ref[:, 3 * D_MODEL:4 * D_MODEL])
    for hd in range(HEADS):
        qkvz_ref[0, 3 * HEADS + hd] = z[:, hd * DH:(hd + 1) * DH].astype(qkvz_ref.dtype)
    hl = (h - hb.astype(F32)).astype(BF16)
    ab = _mm(hb, wabh_ref[...]) + _mm(hl, wabh_ref[...]) + _mm(hb, wabl_ref[...])
    lane = lax.broadcasted_iota(jnp.int32, ab.shape, 1)
    rin = lax.broadcasted_iota(jnp.int32, ab.shape, 0) & (GDN_CHUNK - 1)
    g = -jnp.exp(alog_ref[...]) * _softplus(ab + dtb_ref[...])
    res = jnp.where(lane < HEADS, _chunk_cumsum(g, rin, False, GDN_CHUNK),
                    jnp.where(lane < 2 * HEADS, _chunk_cumsum(g, rin, True, GDN_CHUNK),
                              jax.nn.sigmoid(ab)))
    grow_ref[0] = res.T[0:4 * HEADS, :]


def _gdn_proj(x, nw, mods, w_main, conv_w, wab_hi, wab_lo, alog, dtb):
    B, L, d = x.shape
    tm = _token_tile(L)
    n = w_main.shape[1]
    per_tile = tm // HALO
    last_blk = L // HALO - 1
    x_spec, nw_spec, mod_spec = _proj_in_specs(tm, d)
    prev_spec = pl.BlockSpec((1, HALO, d), lambda b, t: (b, jnp.maximum(t * per_tile - 1, 0), 0))
    next_spec = pl.BlockSpec((1, HALO, d),
                             lambda b, t: (b, jnp.minimum((t + 1) * per_tile, last_blk), 0))
    return pl.pallas_call(
        _gdn_proj_kernel,
        grid=(B, L // tm),
        in_specs=[x_spec, prev_spec, next_spec, nw_spec, mod_spec,
                  pl.BlockSpec(w_main.shape, lambda b, t: (0, 0), pipeline_mode=pl.Buffered(1)),
                  _full(conv_w.shape), _full(wab_hi.shape), _full(wab_lo.shape),
                  _full(alog.shape), _full(dtb.shape)],
        out_specs=[
            pl.BlockSpec((1, n // DH, tm, DH), lambda b, t: (b, 0, t, 0)),
            pl.BlockSpec((1, 4 * HEADS, tm), lambda b, t: (b, 0, t)),
        ],
        out_shape=[
            jax.ShapeDtypeStruct((B, n // DH, L, DH), BF16),
            jax.ShapeDtypeStruct((B, 4 * HEADS, L), F32),
        ],
        compiler_params=_cparams("parallel", "parallel"),
        name="gdn_proj",
    )(x, x, x, nw, mods, w_main, conv_w, wab_hi, wab_lo, alog, dtb)


def _lru_proj_kernel(x_ref, nw_ref, mod_ref, w_ref, gate_ref, xr_ref):
    h = _norm_mod(x_ref[0], nw_ref[...], mod_ref[0, 0:1], mod_ref[0, 1:2])
    hb = h.astype(BF16)
    gate_ref[0] = _gelu_tanh(_mm(hb, w_ref[:, 0:D_MODEL])).astype(gate_ref.dtype)
    xr_ref[0] = _mm(hb, w_ref[:, D_MODEL:2 * D_MODEL])


def _lru_proj(x, nw, mods, w):
    B, L, d = x.shape
    tm = _token_tile(L)
    spec = pl.BlockSpec((1, tm, d), lambda b, t: (b, t, 0))
    return pl.pallas_call(
        _lru_proj_kernel,
        grid=(B, L // tm),
        in_specs=_proj_in_specs(tm, d) + [_full(w.shape)],
        out_specs=[spec, spec],
        out_shape=[jax.ShapeDtypeStruct((B, L, d), BF16), jax.ShapeDtypeStruct((B, L, d), F32)],
        compiler_params=_cparams("parallel", "parallel"),
        name="lru_proj",
    )(x, nw, mods, w)


def _hg_proj_kernel(x_ref, nw_ref, mod_ref, w_ref, lbl_ref, q_ref, f0_ref, f1_ref, v_ref, z_ref,
                    *, layer):
    logits = lbl_ref[...]
    depth = logits.shape[0]
    m = logits[0:1]
    for r in range(1, depth):
        m = jnp.maximum(m, logits[r:r + 1])
    e = jnp.exp(logits - m)
    tot = e[0:1]
    for r in range(1, depth):
        tot = tot + e[r:r + 1]
    lb = jnp.zeros_like(tot)
    for r in range(1, layer + 1):
        lb = lb + e[r:r + 1] / tot

    h = _norm_mod(x_ref[0], nw_ref[...], mod_ref[0, 0:1], mod_ref[0, 1:2])
    hb = h.astype(BF16)

    def cols(c):
        return _mm(hb, w_ref[:, c * D_MODEL:(c + 1) * D_MODEL])

    q_ref[0] = _silu(cols(0)).astype(q_ref.dtype)
    for d, f_ref in enumerate((f0_ref, f1_ref)):
        f = lb + (1.0 - lb) * jax.nn.sigmoid(cols(1 + d))
        f_ref[0] = jnp.log(f)
    v_ref[0] = cols(3).astype(v_ref.dtype)
    z_ref[0] = cols(4).astype(z_ref.dtype)


def _hg_proj(x, nw, mods, w, lb_logits, layer):
    B, L, d = x.shape
    tm = _token_tile(L)
    spec = pl.BlockSpec((1, tm, d), lambda b, t: (b, t, 0))
    sds = lambda dt: jax.ShapeDtypeStruct((B, L, d), dt)
    return pl.pallas_call(
        functools.partial(_hg_proj_kernel, layer=layer),
        grid=(B, L // tm),
        in_specs=_proj_in_specs(tm, d) + [_full(w.shape), _full(lb_logits.shape)],
        out_specs=[spec] * 5,
        out_shape=[sds(BF16), sds(F32), sds(F32), sds(BF16), sds(BF16)],
        compiler_params=_cparams("parallel", "parallel"),
        name="hg_proj",
    )(x, nw, mods, w, lb_logits)


def _post_kernel(x_ref, y_ref, mod_ref, nw_ref, wo_ref, w1_ref, w3_ref, w2_ref, fnw_ref,
                 o_ref, acc_ref, *, final):
    x1 = x_ref[0] + mod_ref[0, 2:3] * _mm(y_ref[0], wo_ref[...])
    hb = _norm_mod(x1, nw_ref[...], mod_ref[0, 3:4], mod_ref[0, 4:5]).astype(BF16)
    hidden = w1_ref.shape[1]
    for j in range(hidden // FFN_COLS):
        cs = slice(j * FFN_COLS, (j + 1) * FFN_COLS)
        a = (_silu(_mm(hb, w1_ref[:, cs])) * _mm(hb, w3_ref[:, cs])).astype(BF16)
        part = _mm(a, w2_ref[cs, :])
        if j == 0:
            acc_ref[...] = part
        else:
            acc_ref[...] += part
    out = x1 + mod_ref[0, 5:6] * acc_ref[...]
    if final:
        out = out * lax.rsqrt(jnp.mean(out * out, axis=-1, keepdims=True) + EPS) * fnw_ref[...]
    o_ref[0] = out


def _post(x, y, mods, nw, wo, w1, w3, w2, final_nw, final):
    B, L, d = x.shape
    tm = _token_tile(L)
    tile = pl.BlockSpec((1, tm, d), lambda b, t: (b, t, 0))

    def resident(shape):
        return pl.BlockSpec(shape, lambda b, t: (0, 0), pipeline_mode=pl.Buffered(1))

    return pl.pallas_call(
        functools.partial(_post_kernel, final=final),
        grid=(B, L // tm),
        in_specs=[tile, tile,
                  pl.BlockSpec((1, 6, d), lambda b, t: (b, 0, 0)),
                  pl.BlockSpec((1, d), lambda b, t: (0, 0)),
                  resident(wo.shape), resident(w1.shape), resident(w3.shape), resident(w2.shape),
                  pl.BlockSpec((1, d), lambda b, t: (0, 0))],
        out_specs=tile,
        out_shape=jax.ShapeDtypeStruct((B, L, d), F32),
        scratch_shapes=[pltpu.VMEM((tm, d), F32)],
        compiler_params=_cparams("parallel", "parallel"),
        name="post_ffn",
    )(x, y, mods, nw, wo, w1, w3, w2, final_nw)


GC = GDN_CHUNK
GDN_GROUP = 8


def _block_diag(m):
    left = lax.broadcasted_iota(jnp.int32, m.shape, 1) < GC
    return jnp.concatenate([jnp.where(left, m, 0.0), jnp.where(left, 0.0, m)], axis=0).astype(BF16)


def _dup_diag(m16):
    z = jnp.zeros_like(m16)
    return jnp.concatenate([jnp.concatenate([m16, z], axis=1),
                            jnp.concatenate([z, m16], axis=1)], axis=0)


def _packed_masks():
    ii = lax.broadcasted_iota(jnp.int32, (GC, 2 * GC), 0)
    jj = lax.broadcasted_iota(jnp.int32, (GC, 2 * GC), 1)
    jl = jj & (GC - 1)
    bwd = jj >= GC
    fwd = jnp.logical_not(bwd)
    incl = (fwd & (ii >= jl)) | (bwd & (ii <= jl))
    strict = (fwd & (ii > jl)) | (bwd & (ii < jl))
    same = {b: (ii & -b) == (jl & -b) for b in (4, 8, 16, 32, 64, 128) if b <= GC}
    return dict(left=fwd, incl=incl, strict=strict, eye=ii == jl, same=same)


def _tri_inverse_packed(nms, masks):
    same = masks["same"]
    n4 = [jnp.where(same[4], n, 0.0) for n in nms]
    n4d = [_block_diag(a) for a in n4]
    p2 = [_mm(a.astype(BF16), d) for a, d in zip(n4, n4d)]
    p3 = [_mm(p.astype(BF16), d) for p, d in zip(p2, n4d)]
    eye = jnp.where(masks["eye"], 1.0, 0.0)
    xs = [eye + a + b + c for a, b, c in zip(n4, p2, p3)]
    b = 4
    while b < GC:
        off_block = same[2 * b] & jnp.logical_not(same[b])
        cms = [jnp.where(off_block, n, 0.0).astype(BF16) for n in nms]
        xds = [_block_diag(x) for x in xs]
        yds = [_block_diag(_mm(c, d)) for c, d in zip(cms, xds)]
        xs = [x + _mm(x.astype(BF16), d) for x, d in zip(xs, yds)]
        b *= 2
    return xs


def _gdn_precompute(q_ref, k_ref, v_ref, gp_ref, base, n_chunks,
                    mp_s, cp_s, qp_s, op_s, dl_s):
    masks = _packed_masks()
    left, incl, strict = masks["left"], masks["incl"], masks["strict"]
    grp = min(GDN_GROUP, n_chunks)

    def group(t0):
        ts = [t0 + g for g in range(grp)]
        tiles = [gp_ref[0, 0, t] for t in ts]
        cols = [tl.T for tl in tiles]
        qs, ks, v16s = [], [], []
        for t in ts:
            r = _rows(t, 0, GC)
            q = _silu(q_ref[0, 0, r, :].astype(F32))
            qs.append(q * lax.rsqrt(jnp.sum(q * q, axis=-1, keepdims=True) + EPS) * (DH ** -0.5))
            k = _silu(k_ref[0, 0, r, :].astype(F32))
            ks.append(k * lax.rsqrt(jnp.sum(k * k, axis=-1, keepdims=True) + EPS))
            v16s.append(_silu(v_ref[0, 0, r, :].astype(F32)).astype(BF16))
        k16s = [k.astype(BF16) for k in ks]
        q16s = [q.astype(BF16) for q in qs]
        kqs = [_mm_nt(jnp.concatenate([k, q], axis=0), jnp.concatenate([k, k], axis=0))
               for k, q in zip(k16s, q16s)]
        g_rows = [tl[0:1, :] for tl in tiles]
        b_rows = [tl[1:2, :] for tl in tiles]
        g_cols = [jnp.where(left, c[0:GC, 0:1], c[GC:2 * GC, 0:1]) for c in cols]
        b_cols = [jnp.where(left, c[0:GC, 1:2], c[GC:2 * GC, 1:2]) for c in cols]
        decays = [jnp.where(incl, jnp.exp(jnp.where(incl, gc - gr, 0.0)), 0.0)
                  for gc, gr in zip(g_cols, g_rows)]
        nms = [jnp.where(strict, -(kq[0:GC] * bc * dc), 0.0)
               for kq, bc, dc in zip(kqs, b_cols, decays)]
        tinvs = _tri_inverse_packed(nms, masks)
        tus = [(x * br).astype(BF16) for x, br in zip(tinvs, b_rows)]
        tws = [(x * (br * jnp.exp(gr))).astype(BF16) for x, br, gr in zip(tinvs, b_rows, g_rows)]
        uus = [_mm(tu, _dup_diag(v)) for tu, v in zip(tus, v16s)]
        wws = [_mm(tw, _dup_diag(k)) for tw, k in zip(tws, k16s)]
        u16s = [u.astype(BF16) for u in uus]
        w16s = [w.astype(BF16) for w in wws]
        a16s = [jnp.where(incl, kq[GC:2 * GC] * dc, 0.0).astype(BF16)
                for kq, dc in zip(kqs, decays)]
        zero = jnp.zeros((GC, DH), BF16)
        aus = [_mm(a, jnp.concatenate(
            [jnp.concatenate([w[:, 0:DH], u[:, 0:DH], zero, zero], axis=1),
             jnp.concatenate([zero, zero, w[:, DH:2 * DH], u[:, DH:2 * DH]], axis=1)], axis=0))
            for a, w, u in zip(a16s, w16s, u16s)]
        g_tots = [(tl[0:1, GC - 1:GC], tl[0:1, GC:GC + 1]) for tl in tiles]
        g_colds = [(c[0:GC, 0:1], c[GC:2 * GC, 0:1]) for c in cols]
        kes = [[(k * jnp.exp(gt[d] - gc[d])).astype(BF16) for d in range(2)]
               for k, gt, gc in zip(ks, g_tots, g_colds)]
        kwus = [[_mm_tn(ke[d], jnp.concatenate([w[:, d * DH:(d + 1) * DH],
                                                u[:, d * DH:(d + 1) * DH]], axis=1))
                 for d in range(2)] for ke, w, u in zip(kes, w16s, u16s)]
        for g, t in enumerate(ts):
            dst = _rows(t, base, GC)
            ci = base // GC + t
            dst2 = pl.ds(ci * DH, DH) if isinstance(ci, int) else \
                pl.ds(pl.multiple_of(ci * DH, DH), DH)
            for d in range(2):
                qe = qs[g] * jnp.exp(g_colds[g][d])
                qp_s[d, dst, :] = (qe - aus[g][:, 2 * d * DH:(2 * d + 1) * DH]).astype(BF16)
                op_s[d, dst, :] = aus[g][:, (2 * d + 1) * DH:(2 * d + 2) * DH]
                mp_s[d, dst2, :] = kwus[g][d][:, 0:DH].astype(BF16)
                cp_s[d, dst2, :] = kwus[g][d][:, DH:2 * DH].astype(BF16)
                dl_s[d, pl.ds(ci, 1), :] = jnp.broadcast_to(jnp.exp(g_tots[g][d]), (1, DH))

    n_groups = n_chunks // grp
    group(0)
    if n_groups > 2:
        def body(i, carry):
            group(i * grp)
            return carry
        lax.fori_loop(1, n_groups - 1, body, 0)
    if n_groups > 1:
        group((n_groups - 1) * grp)


def _gdn_mixer_kernel(qc, kc, vc, zc, gpc, ql, kl, vl, zl, gpl, nw, yc, yl,
                      mp_s, cp_s, qp_s, op_s, dl_s, oh, sf, sb, *, Lc, Ll):
    nc, nl = Lc // GC, Ll // GC
    scr = (mp_s, cp_s, qp_s, op_s, dl_s)
    _gdn_precompute(qc, kc, vc, gpc, 0, nc, *scr)
    _gdn_precompute(ql, kl, vl, gpl, Lc, nl, *scr)
    sf[...] = jnp.zeros_like(sf)
    sb[...] = jnp.zeros_like(sb)
    nw_v = nw[...]
    s_refs = (sf, sb)

    def scan(z_ref, y_ref, base, n):
        def steps(cs):
            rbs = [_rows(c, base, GC) for c in cs]
            cis = [base // GC + c for c in cs]
            rms = [_rows(ci, 0, DH) for ci in cis]
            ss = [r[...] for r in s_refs]
            res = [_mm(jnp.concatenate([mp_s[d, rms[d], :], qp_s[d, rbs[d], :]], axis=0),
                       ss[d].astype(BF16)) for d in range(2)]
            outs = [res[d][DH:DH + GC] + op_s[d, rbs[d], :] for d in range(2)]
            for d in range(2):
                dl = dl_s[d, pl.ds(cis[d], 1), :]
                s_refs[d][...] = ss[d] * dl - res[d][0:DH] + cp_s[d, rms[d], :].astype(F32)
            return rbs, outs

        def first_visit(cs):
            rbs, outs = steps(cs)
            for d in range(2):
                oh[rbs[d], :] = outs[d]

        def second_visit(cs):
            rbs, outs = steps(cs)
            for d in range(2):
                oh[rbs[d], :] = outs[d] + oh[rbs[d], :]

        def finish(cs):
            for c in cs:
                o = oh[_rows(c, base, GC), :]
                y = o * lax.rsqrt(jnp.mean(o * o, axis=-1, keepdims=True) + EPS) * nw_v
                r = _rows(c, 0, GC)
                y_ref[0, r, :] = (y * _silu(z_ref[0, 0, r, :].astype(F32))).astype(y_ref.dtype)

        def first_body(i, carry):
            first_visit((i, n - 1 - i))
            return carry

        def second_body(i, carry):
            finish((i - 1, n - i))
            second_visit((i, n - 1 - i))
            return carry

        half = n // 2
        lax.fori_loop(0, half, first_body, 0)
        second_visit((half, n - 1 - half))
        lax.fori_loop(half + 1, n, second_body, 0)
        finish((n - 1, 0))

    scan(zc, yc, 0, nc)
    scan(zl, yl, Lc, nl)


def _gdn_mixer(pc, grc, plat, grl, norm_w):
    B, _, Lc, _ = pc.shape
    Ll = plat.shape[2]

    def pack_rows(gr, L):
        g = gr.reshape(B, 2, 2, HEADS, L // GC, GC).transpose(0, 3, 4, 1, 2, 5)
        g = g.reshape(B, HEADS, L // GC, 2, 2 * GC)
        return jnp.pad(g, ((0, 0), (0, 0), (0, 0), (0, 6), (0, 0)))

    def seq_specs(L, mode):
        head_blk = lambda k: pl.BlockSpec((1, 1, L, DH), lambda b, h, k=k: (b, k * HEADS + h, 0, 0),
                                          pipeline_mode=mode)
        return [head_blk(0), head_blk(1), head_blk(2), head_blk(3),
                pl.BlockSpec((1, 1, L // GC, 8, 2 * GC), lambda b, h: (b, h, 0, 0, 0))]

    y_spec = lambda L: pl.BlockSpec((1, L, DH), lambda b, h: (b, 0, h))
    Lt = Lc + Ll
    per_dir = lambda w, dt: pltpu.VMEM((2, Lt, w), dt)
    return pl.pallas_call(
        functools.partial(_gdn_mixer_kernel, Lc=Lc, Ll=Ll),
        grid=(B, HEADS),
        in_specs=seq_specs(Lc, None) + seq_specs(Ll, pl.Buffered(1)) + [
            pl.BlockSpec((1, DH), lambda b, h: (0, 0))],
        out_specs=[y_spec(Lc), y_spec(Ll)],
        out_shape=[jax.ShapeDtypeStruct((B, Lc, HEADS * DH), BF16),
                   jax.ShapeDtypeStruct((B, Ll, HEADS * DH), BF16)],
        scratch_shapes=[pltpu.VMEM((2, Lt // GC * DH, DH), BF16),
                        pltpu.VMEM((2, Lt // GC * DH, DH), BF16),
                        per_dir(DH, BF16), per_dir(DH, F32),
                        pltpu.VMEM((2, Lt // GC, DH), F32),
                        pltpu.VMEM((Lt, DH), F32), pltpu.VMEM((DH, DH), F32),
                        pltpu.VMEM((DH, DH), F32)],
        compiler_params=_cparams("parallel", "arbitrary"),
        name="gdn_mixer",
    )(pc, pc, pc, pc, pack_rows(grc, Lc), plat, plat, plat, plat, pack_rows(grl, Ll),
      norm_w)


SCAN_GROUP = 8


def _hg_chunk_parts(q16, logf, v16, rev):
    rows = lax.broadcasted_iota(jnp.int32, (CHUNK, DH), 0)
    gcum = _chunk_cumsum(logf, rows, rev)
    mid = CHUNK // 2 - 1 if rev else CHUNK // 2
    end = 0 if rev else CHUNK - 1
    ref = gcum[mid:mid + 1, :]
    glast = gcum[end:end + 1, :]
    q = q16.astype(F32)
    k = 1.0 - jnp.exp(logf)
    ii = lax.broadcasted_iota(jnp.int32, (CHUNK, CHUNK), 0)
    jj = lax.broadcasted_iota(jnp.int32, (CHUNK, CHUNK), 1)
    incl = (ii <= jj) if rev else (ii >= jj)
    scores = _mm_nt((q * jnp.exp(gcum - ref)).astype(BF16), (k * jnp.exp(ref - gcum)).astype(BF16))
    intra = _mm(jnp.where(incl, scores, 0.0).astype(BF16), v16)
    qd = (q * jnp.exp(gcum)).astype(BF16)
    upd = _mm_tn(v16, (k * jnp.exp(glast - gcum)).astype(BF16))
    return intra, qd, upd, jnp.exp(glast)


def _hg_mixer_kernel(qc, f0c, f1c, vc, zc, ql, f0l, f1l, vl, zl, nw, yc, yl, oh, sf, sb,
                     *, Lc, Ll):
    nc, nl = Lc // CHUNK, Ll // CHUNK
    sf[...] = jnp.zeros_like(sf)
    sb[...] = jnp.zeros_like(sb)
    nw_v = nw[...]

    def scan(q_ref, f_refs, v_ref, z_ref, y_ref, base, n):
        grp = min(SCAN_GROUP, n)
        n_groups = n // grp

        def advance(i):
            res = []
            for d, s_ref in enumerate((sf, sb)):
                cs = [i * grp + g for g in range(grp)] if d == 0 else \
                    [n - 1 - i * grp - g for g in range(grp)]
                parts = [_hg_chunk_parts(q_ref[0, _rows(c), :], f_refs[d][0, _rows(c), :],
                                         v_ref[0, _rows(c), :], d == 1) for c in cs]
                st = s_ref[...]
                outs = []
                for intra, qd, upd, dl in parts:
                    outs.append(intra + _mm_nt(qd, st.astype(BF16)))
                    st = st * dl + upd
                s_ref[...] = st
                res.append((cs, outs))
            return res

        def finish(c, o):
            y = o * lax.rsqrt(jnp.mean(o * o, axis=-1, keepdims=True) + EPS) * nw_v
            r = _rows(c)
            y_ref[0, r, :] = (y * _silu(z_ref[0, r, :].astype(F32))).astype(y_ref.dtype)

        def rb(c):
            return _rows(c, base)

        def first_visit(i, carry):
            for cs, outs in advance(i):
                for c, o in zip(cs, outs):
                    oh[rb(c), :] = o
            return carry

        def second_visit(i, carry):
            for cs, outs in advance(i):
                for c, o in zip(cs, outs):
                    finish(c, o + oh[rb(c), :])
            return carry

        half = n_groups // 2
        lax.fori_loop(0, half, first_visit, 0)
        if n_groups % 2:
            (cs_f, outs_f), (cs_b, outs_b) = advance(half)
            for c, o, ob in zip(cs_f, outs_f, reversed(outs_b)):
                finish(c, o + ob)
        lax.fori_loop(n_groups - half, n_groups, second_visit, 0)

    scan(qc, (f0c, f1c), vc, zc, yc, 0, nc)
    scan(ql, (f0l, f1l), vl, zl, yl, Lc, nl)


def _hg_mixer(ctx_parts, lat_parts, norm_w):
    B, Lc, _ = ctx_parts[0].shape
    Ll = lat_parts[0].shape[1]
    blk = lambda L: pl.BlockSpec((1, L, DH), lambda b, h: (b, 0, h))
    Lt = Lc + Ll
    return pl.pallas_call(
        functools.partial(_hg_mixer_kernel, Lc=Lc, Ll=Ll),
        grid=(B, HEADS),
        in_specs=[blk(Lc)] * 5 + [blk(Ll)] * 5 + [pl.BlockSpec((1, DH), lambda b, h: (0, 0))],
        out_specs=[blk(Lc), blk(Ll)],
        out_shape=[jax.ShapeDtypeStruct((B, Lc, HEADS * DH), BF16),
                   jax.ShapeDtypeStruct((B, Ll, HEADS * DH), BF16)],
        scratch_shapes=[pltpu.VMEM((Lt, DH), F32)] + [pltpu.VMEM((DH, DH), F32)] * 2,
        compiler_params=_cparams("parallel", "arbitrary"),
        name="hg_mixer",
    )(*ctx_parts, *lat_parts, norm_w)


def _lru_chunk_parts(x, w_ref, b_ref, sp, d):
    rev = d == 1
    cs = slice(2 * DH * d, 2 * DH * (d + 1))
    gates = _mm(x.astype(BF16), w_ref[0, :, cs]) + b_ref[0, :, cs]
    r = jax.nn.sigmoid(gates[:, 0:DH])
    ig = jax.nn.sigmoid(gates[:, DH:2 * DH])
    log_a = -LRU_C * r * sp
    a = jnp.exp(log_a)
    bv = jnp.sqrt(1.0 - jnp.exp(2.0 * log_a)) * (ig * x)
    rows = lax.broadcasted_iota(jnp.int32, (CHUNK, DH), 0)
    k = 1
    while k < CHUNK:
        if rev:
            m = rows < CHUNK - k
            a_sh, b_sh = _shift_rows(a, k), _shift_rows(bv, k)
        else:
            m = rows >= k
            a_sh, b_sh = _shift_rows(a, -k), _shift_rows(bv, -k)
        bv = jnp.where(m, bv + a * b_sh, bv)
        a = jnp.where(m, a * a_sh, a)
        k *= 2
    return a, bv


def _lru_mixer_kernel(xc, gc, xl, gl, cw, cb, w4, b4, lam, yc, yl, xs, of, ob, *, Lc, Ll):
    nc, nl = Lc // CHUNK, Ll // CHUNK
    cw_v = cw[...]
    cb_v = cb[...]

    def prep(x_ref, base, n):
        def fn(t):
            dst = pl.ds(base + t * CHUNK, CHUNK) if isinstance(t, int) else \
                pl.ds(pl.multiple_of(base + t * CHUNK, CHUNK), CHUNK)
            xs[dst, :] = _conv_chunk(x_ref, t, n, cw_v) + cb_v
        _for_chunks(n, fn)

    prep(xc, 0, nc)
    prep(xl, Lc, nl)
    sp0 = _softplus(-lam[0:1, :])
    sp1 = _softplus(-lam[1:2, :])

    def scan(base, n, carry):
        grp = min(SCAN_GROUP, n)

        def body(i, c):
            hs = list(c)
            for d, (sp, o_ref) in enumerate(((sp0, of), (sp1, ob))):
                cs = [i * grp + g for g in range(grp)] if d == 0 else \
                    [n - 1 - i * grp - g for g in range(grp)]
                parts = [_lru_chunk_parts(xs[_rows(t, base), :], w4, b4, sp, d) for t in cs]
                end = CHUNK - 1 if d == 0 else 0
                for t, (a_cum, b_cum) in zip(cs, parts):
                    hrows = b_cum + a_cum * hs[d]
                    o_ref[_rows(t, base), :] = hrows
                    hs[d] = hrows[end:end + 1, :]
            return tuple(hs)
        return lax.fori_loop(0, n // grp, body, carry)

    zero = jnp.zeros((1, DH), F32)
    carry = scan(0, nc, (zero, zero))
    scan(Lc, nl, carry)

    def out(g_ref, y_ref, base, n):
        def body(t, c):
            r = _rows(t)
            rb = pl.ds(pl.multiple_of(base + t * CHUNK, CHUNK), CHUNK)
            y_ref[0, r, :] = (g_ref[0, r, :].astype(F32) * (of[rb, :] + ob[rb, :])).astype(y_ref.dtype)
            return c
        lax.fori_loop(0, n, body, 0)

    out(gc, yc, 0, nc)
    out(gl, yl, Lc, nl)


def _lru_mixer(xr_c, gate_c, xr_l, gate_l, conv_w, conv_b, w4, b4, lam):
    B, Lc, _ = xr_c.shape
    Ll = xr_l.shape[1]
    blk = lambda L: pl.BlockSpec((1, L, DH), lambda b, g: (b, 0, g))
    Lt = Lc + Ll
    return pl.pallas_call(
        functools.partial(_lru_mixer_kernel, Lc=Lc, Ll=Ll),
        grid=(B, HEADS),
        in_specs=[blk(Lc), blk(Lc), blk(Ll), blk(Ll),
                  pl.BlockSpec((4, DH), lambda b, g: (0, g)),
                  pl.BlockSpec((1, DH), lambda b, g: (0, g)),
                  pl.BlockSpec((1, DH, 4 * DH), lambda b, g: (g, 0, 0)),
                  pl.BlockSpec((1, 1, 4 * DH), lambda b, g: (g, 0, 0)),
                  pl.BlockSpec((2, DH), lambda b, g: (0, g))],
        out_specs=[blk(Lc), blk(Ll)],
        out_shape=[jax.ShapeDtypeStruct((B, Lc, HEADS * DH), BF16),
                   jax.ShapeDtypeStruct((B, Ll, HEADS * DH), BF16)],
        scratch_shapes=[pltpu.VMEM((Lt, DH), F32)] * 3,
        compiler_params=_cparams("parallel", "arbitrary"),
        name="lru_mixer",
    )(xr_c, gate_c, xr_l, gate_l, conv_w, conv_b, w4, b4, lam)


def _to_scan_order(h):
    bsz, L, d = h.shape
    return h.reshape(bsz, L // GRID_W, GRID_W, d).transpose(0, 2, 1, 3).reshape(bsz, L, d)


def _from_scan_order(h):
    bsz, L, d = h.shape
    return h.reshape(bsz, GRID_W, L // GRID_W, d).transpose(0, 2, 1, 3).reshape(bsz, L, d)


def _pad_lanes(v, layout):
    row = jnp.zeros((DH,), F32)
    for off in layout:
        row = row.at[off:off + v.shape[0]].set(v)
    return row.reshape(1, DH)


def kernel(x, c, ctx, c_ctx, ada_w, ada_b, norm_mix, norm_ffn, norm_final, ffn_w1, ffn_w3, ffn_w2, gdn_w_in, gdn_conv, gdn_a_log, gdn_dt_bias, gdn_norm, gdn_w_out, lru_w_in, lru_conv_w, lru_conv_b, lru_w_r, lru_b_r, lru_w_i, lru_b_i, lru_lambda, lru_w_out, hg_w_in, hg_lb_logits, hg_norm, hg_w_out):
    B, L, d = x.shape
    depth = ada_w.shape[0]
    assert d == D_MODEL and L % GRID_W == 0 and L % GDN_CHUNK == 0 and ctx.shape[1] % GDN_CHUNK == 0
    assert B + 1 <= 8

    cvec = jnp.zeros((8, d), F32).at[0:B].set(c).at[B].set(c_ctx)
    mods = _mods(cvec, ada_w, ada_b).reshape(depth, 8, 6, d)

    xl, xc = x, ctx
    for i in range(depth):
        mod_l = mods[i, 0:B]
        mod_c = jnp.broadcast_to(mods[i, B][None], (B, 6, d))
        col = i % 2 == 1
        if col:
            xl = _to_scan_order(xl)
        nw = norm_mix[i].reshape(1, d)
        kind, j = i % 3, i // 3
        if kind == 0:
            w_in = gdn_w_in[j]
            n_main = 4 * HEADS * DH
            w_main = w_in[:, :n_main].astype(BF16)
            wa = w_in[:, n_main:n_main + 2 * HEADS]
            wb = w_in[:, n_main + 2 * HEADS:n_main + 4 * HEADS]
            wab = jnp.zeros((d, DH), F32)
            wab = wab.at[:, 0:2 * HEADS].set(wa).at[:, 2 * HEADS:4 * HEADS].set(wb)
            wab_hi = wab.astype(BF16)
            wab_lo = (wab - wab_hi.astype(F32)).astype(BF16)
            alog = _pad_lanes(gdn_a_log[j].reshape(-1), (0,))
            dtb = _pad_lanes(gdn_dt_bias[j].reshape(-1), (0,))
            pc, grc = _gdn_proj(xc, nw, mod_c, w_main, gdn_conv[j], wab_hi, wab_lo, alog, dtb)
            plat, grl = _gdn_proj(xl, nw, mod_l, w_main, gdn_conv[j], wab_hi, wab_lo, alog, dtb)
            yc, yl = _gdn_mixer(pc, grc, plat, grl, gdn_norm[j].reshape(1, DH))
            w_out = gdn_w_out[j]
        elif kind == 1:
            w_in = lru_w_in[j].astype(BF16)
            gate_c, xr_c = _lru_proj(xc, nw, mod_c, w_in)
            gate_l, xr_l = _lru_proj(xl, nw, mod_l, w_in)
            w4 = jnp.concatenate([lru_w_r[j, 0], lru_w_i[j, 0], lru_w_r[j, 1], lru_w_i[j, 1]],
                                 axis=-1).astype(BF16)
            b4 = jnp.stack([lru_b_r[j, 0], lru_b_i[j, 0], lru_b_r[j, 1], lru_b_i[j, 1]], axis=0)
            b4 = b4.reshape(4, HEADS, DH).transpose(1, 0, 2).reshape(HEADS, 1, 4 * DH)
            yc, yl = _lru_mixer(xr_c, gate_c, xr_l, gate_l, lru_conv_w[j],
                                lru_conv_b[j].reshape(1, d), w4, b4, lru_lambda[j])
            w_out = lru_w_out[j]
        else:
            w_in = hg_w_in[j].astype(BF16)
            parts_c = _hg_proj(xc, nw, mod_c, w_in, hg_lb_logits, i)
            parts_l = _hg_proj(xl, nw, mod_l, w_in, hg_lb_logits, i)
            yc, yl = _hg_mixer(parts_c, parts_l, hg_norm[j].reshape(1, DH))
            w_out = hg_w_out[j]

        nwf = norm_ffn[i].reshape(1, d)
        wo, w1, w3, w2 = (w.astype(BF16) for w in (w_out, ffn_w1[i], ffn_w3[i], ffn_w2[i]))
        last = i == depth - 1
        fnw = norm_final.reshape(1, d)
        xl = _post(xl, yl, mod_l, nwf, wo, w1, w3, w2, fnw, last)
        if not last:
            xc = _post(xc, yc, mod_c, nwf, wo, w1, w3, w2, fnw, False)
        if col:
            xl = _from_scan_order(xl)
    return xl
```
